```python
import math
import jax, jax.numpy as jnp
from jax import lax
import numpy as np

D_MODEL = 1024
BATCH = 2
SEQ = 8192
DEPTH = 2
DEC_BATCH = 128
DEC_SEQ = 1
PAST_LEN = 8192
PAGE_SIZE = 128

POOL_WINDOWS = (2, 4, 8, 16)
POOL_GROUPS = 4
POOL_WIDTH = D_MODEL // 2
POOL_GC = POOL_WIDTH // POOL_GROUPS
POOL_WMAX = 16
POOL_BUF = POOL_WMAX - 1
N_HEADS = 8
QK_NOPE = 64
QK_ROPE = 32
QK_HEAD = QK_NOPE + QK_ROPE
V_DIM = 64
Q_LORA = 384
KV_LORA = 256
ATTN_WIDTH = N_HEADS * V_DIM
ROPE_THETA = 10000.0
SM_SCALE = QK_HEAD ** -0.5
Q_BLOCK = 128
PLE_DIM = 256
DN_ALPHA = (2 * DEPTH) ** 0.25
DN_BETA = (8 * DEPTH) ** -0.25
NORM_EPS = 1e-6
IN_SPLITS = (POOL_WIDTH, POOL_WIDTH, Q_LORA, KV_LORA, QK_ROPE, ATTN_WIDTH, D_MODEL, D_MODEL)
IN_WIDTH = sum(IN_SPLITS)

kernel_name = 'hybrid_pool_mla_deepnorm_decoder_step'


def layer_norm(x, g, b):
    xf = x.astype(jnp.float32)
    mu = jnp.mean(xf, -1, keepdims=True)
    var = jnp.mean(jnp.square(xf - mu), -1, keepdims=True)
    return ((xf - mu) * lax.rsqrt(var + NORM_EPS) * g + b).astype(x.dtype)


def rms_norm(x, g):
    xf = x.astype(jnp.float32)
    return (xf * lax.rsqrt(jnp.mean(xf * xf, -1, keepdims=True) + NORM_EPS) * g).astype(x.dtype)


def rope(x, pos):
    half = QK_ROPE // 2
    freqs = ROPE_THETA ** (-jnp.arange(half, dtype=jnp.float32) / half)
    ang = pos.astype(jnp.float32)[:, None] * freqs[None, :]
    cos = jnp.cos(ang)[:, None, :]
    sin = jnp.sin(ang)[:, None, :]
    xf = x.astype(jnp.float32)
    x1, x2 = xf[..., :half], xf[..., half:]
    return jnp.concatenate([x1 * cos - x2 * sin, x1 * sin + x2 * cos], -1).astype(x.dtype)


def split_in(u):
    cuts = np.cumsum(np.array(IN_SPLITS))[:-1].tolist()
    return jnp.split(u, cuts, axis=-1)


def pool_mix(v_ext, pos_out):
    L = v_ext.shape[1]
    T = pos_out.shape[0]
    cs = jnp.cumsum(v_ext.astype(jnp.float32), axis=1)
    csp = jnp.pad(cs, ((0, 0), (POOL_WMAX, 0), (0, 0)))
    hi = csp[:, L - T + POOL_WMAX: L + POOL_WMAX]
    outs = []
    for g, w in enumerate(POOL_WINDOWS):
        sl = slice(g * POOL_GC, (g + 1) * POOL_GC)
        lo = csp[:, L - T + POOL_WMAX - w: L + POOL_WMAX - w, sl]
        cnt = jnp.minimum(pos_out + 1, w).astype(jnp.float32)[None, :, None]
        outs.append((hi[..., sl] - lo) / cnt)
    mean = jnp.concatenate(outs, -1)
    return (mean - v_ext[:, L - T:].astype(jnp.float32)).astype(v_ext.dtype)


def pool_branch(v_ext, pos_out, gate, pool_w, pool_scale):
    m = pool_mix(v_ext, pos_out)
    B, T, _ = m.shape
    m = m.reshape(B, T, POOL_GROUPS, POOL_GC)
    y = jnp.einsum('btgc,gcd->btgd', m, pool_w).reshape(B, T, POOL_WIDTH)
    return y * pool_scale * jax.nn.silu(gate)


def mla_qk(c_q, c_kv, k_r, pos, q_norm_g, w_uq, kv_norm_g):
    B, T, _ = c_q.shape
    q = (rms_norm(c_q, q_norm_g) @ w_uq).reshape(B, T, N_HEADS, QK_HEAD)
    q_nope = q[..., :QK_NOPE]
    q_rope = rope(q[..., QK_NOPE:], pos)
    lat = rms_norm(c_kv, kv_norm_g)
    k_rope = rope(k_r[:, :, None, :], pos)[:, :, 0, :]
    return q_nope, q_rope, lat, k_rope


def mla_prompt(q_nope, q_rope, lat, k_rope, w_uk, w_uv):
    B, S = lat.shape[0], lat.shape[1]
    k_nope = jnp.einsum('bsc,chd->bshd', lat, w_uk)
    v = jnp.einsum('bsc,chd->bshd', lat, w_uv)
    nb = S // Q_BLOCK
    k_pos = jnp.arange(S)

    def one_block(args):
        qn, qr, q0 = args
        s = (jnp.einsum('bqhd,bkhd->bhqk', qn, k_nope)
             + jnp.einsum('bqhr,bkr->bhqk', qr, k_rope)).astype(jnp.float32) * SM_SCALE
        q_pos = q0 + jnp.arange(Q_BLOCK)
        s = jnp.where(k_pos[None, :] <= q_pos[:, None], s, -jnp.inf)
        p = jax.nn.softmax(s, axis=-1).astype(v.dtype)
        return jnp.einsum('bhqk,bkhd->bqhd', p, v)

    qn_b = q_nope.reshape(B, nb, Q_BLOCK, N_HEADS, QK_NOPE).transpose(1, 0, 2, 3, 4)
    qr_b = q_rope.reshape(B, nb, Q_BLOCK, N_HEADS, QK_ROPE).transpose(1, 0, 2, 3, 4)
    starts = jnp.arange(nb) * Q_BLOCK
    o = lax.map(one_block, (qn_b, qr_b, starts))
    return o.transpose(1, 0, 2, 3, 4).reshape(B, S, ATTN_WIDTH)


def mla_sample(q_nope, q_rope, lat, k_rope, past_lat, past_rope, w_uk, w_uv):
    B, T = lat.shape[0], lat.shape[1]
    P = past_lat.shape[1]
    q_lat = jnp.einsum('bthd,chd->bthc', q_nope, w_uk)
    s_past = (jnp.einsum('bthc,bkc->bhtk', q_lat, past_lat)
              + jnp.einsum('bthr,bkr->bhtk', q_rope, past_rope)).astype(jnp.float32) * SM_SCALE
    s_new = (jnp.einsum('bthc,bkc->bhtk', q_lat, lat)
             + jnp.einsum('bthr,bkr->bhtk', q_rope, k_rope)).astype(jnp.float32) * SM_SCALE
    causal = jnp.tril(jnp.ones((T, T), dtype=bool))
    s_new = jnp.where(causal, s_new, -jnp.inf)
    p = jax.nn.softmax(jnp.concatenate([s_past, s_new], -1), axis=-1).astype(lat.dtype)
    o_lat = (jnp.einsum('bhtk,bkc->bthc', p[..., :P], past_lat)
             + jnp.einsum('bhtk,bkc->bthc', p[..., P:], lat))
    o = jnp.einsum('bthc,chd->bthd', o_lat, w_uv)
    return o.reshape(B, T, ATTN_WIDTH)


def trunk_layer(x, p_i, pos, pool_prefix, past, w_in, pool_w, pool_scale, pool_up, q_norm_g, w_uq,
                kv_norm_g, w_uk, w_uv, attn_up, w_out, ple_proj, ple_gate, ln_g, ln_b):
    pv, pg, c_q, c_kv, k_r, ag, mg_pool, mg_attn = split_in(x @ w_in)
    v_ext = pv if pool_prefix is None else jnp.concatenate([pool_prefix, pv], axis=1)
    b_pool = pool_branch(v_ext, pos, pg, pool_w, pool_scale) @ pool_up
    q_nope, q_rope, lat, k_rope = mla_qk(c_q, c_kv, k_r, pos, q_norm_g, w_uq, kv_norm_g)
    if past is None:
        o = mla_prompt(q_nope, q_rope, lat, k_rope, w_uk, w_uv)
    else:
        o = mla_sample(q_nope, q_rope, lat, k_rope, past[0], past[1], w_uk, w_uv)
    b_attn = (o * jax.nn.silu(ag)) @ attn_up
    sub = (jax.nn.sigmoid(mg_pool) * b_pool + jax.nn.sigmoid(mg_attn) * b_attn) @ w_out
    r = DN_ALPHA * x + sub + jax.nn.sigmoid(x @ ple_gate) * (p_i @ ple_proj)
    return layer_norm(r, ln_g, ln_b), lat, k_rope, v_ext[:, -POOL_BUF:]


def setup_inputs(seed: int = 0) -> dict:
    key = jax.random.key(seed)
    ks = jax.random.split(key, 32)
    f32 = jnp.float32
    n_pages = PAST_LEN // PAGE_SIZE
    n_phys = (DEC_BATCH * n_pages * 5) // 4

    def nrm(k, shape, scale):
        return jax.random.normal(k, shape, f32) * scale

    def gain(k, shape):
        return 1.0 + 0.1 * jax.random.normal(k, shape, f32)

    page_table = jax.random.permutation(ks[7], n_phys)[: DEC_BATCH * n_pages].reshape(
        DEC_BATCH, n_pages).astype(jnp.int32)
    return {
        'x_prompt': nrm(ks[0], (BATCH, SEQ, D_MODEL), 1.0),
        'x_sample': nrm(ks[1], (DEC_BATCH, DEC_SEQ, D_MODEL), 1.0),
        'cache_kv_latent': nrm(ks[2], (DEPTH, n_phys, PAGE_SIZE, KV_LORA), 1.0),
        'cache_k_rope': nrm(ks[3], (DEPTH, n_phys, PAGE_SIZE, QK_ROPE), 1.0),
        'state_pool': nrm(ks[4], (DEPTH, DEC_BATCH, POOL_BUF, POOL_WIDTH), 1.0),
        'page_table': page_table,
        'p_prompt': nrm(ks[5], (DEPTH, BATCH, SEQ, PLE_DIM), 1.0),
        'p_sample': nrm(ks[6], (DEPTH, DEC_BATCH, DEC_SEQ, PLE_DIM), 1.0),
        'ln_in_g': gain(ks[8], (D_MODEL,)),
        'ln_in_b': nrm(ks[9], (D_MODEL,), 0.02),
        'w_in': nrm(ks[10], (DEPTH, D_MODEL, IN_WIDTH), D_MODEL ** -0.5),
        'pool_w': nrm(ks[11], (DEPTH, POOL_GROUPS, POOL_GC, POOL_GC), POOL_GC ** -0.5),
        'pool_scale': gain(ks[12], (DEPTH, POOL_WIDTH)),
        'pool_up': nrm(ks[13], (DEPTH, POOL_WIDTH, D_MODEL), DN_BETA * POOL_WIDTH ** -0.5),
        'q_norm_g': gain(ks[14], (DEPTH, Q_LORA)),
        'w_uq': nrm(ks[15], (DEPTH, Q_LORA, N_HEADS * QK_HEAD), Q_LORA ** -0.5),
        'kv_norm_g': gain(ks[16], (DEPTH, KV_LORA)),
        'w_uk': nrm(ks[17], (DEPTH, KV_LORA, N_HEADS, QK_NOPE), KV_LORA ** -0.5),
        'w_uv': nrm(ks[18], (DEPTH, KV_LORA, N_HEADS, V_DIM), KV_LORA ** -0.5),
        'attn_up': nrm(ks[19], (DEPTH, ATTN_WIDTH, D_MODEL), DN_BETA * ATTN_WIDTH ** -0.5),
        'w_out': nrm(ks[20], (DEPTH, D_MODEL, D_MODEL), DN_BETA * D_MODEL ** -0.5),
        'ple_proj': nrm(ks[21], (DEPTH, PLE_DIM, D_MODEL), PLE_DIM ** -0.5),
        'ple_gate': nrm(ks[22], (DEPTH, D_MODEL, D_MODEL), D_MODEL ** -0.5),
        'ln_g': gain(ks[23], (DEPTH, D_MODEL)),
        'ln_b': nrm(ks[24], (DEPTH, D_MODEL), 0.02),
    }


def reference(x_prompt, x_sample, cache_kv_latent, cache_k_rope, state_pool, page_table, p_prompt, p_sample,
              ln_in_g, ln_in_b, w_in, pool_w, pool_scale, pool_up, q_norm_g, w_uq, kv_norm_g, w_uk, w_uv,
              attn_up, w_out, ple_proj, ple_gate, ln_g, ln_b):
    pos_p = jnp.arange(SEQ)
    pos_s = PAST_LEN + jnp.arange(DEC_SEQ)
    n_seq = page_table.shape[0]
    hp = layer_norm(x_prompt, ln_in_g, ln_in_b)
    hs = layer_norm(x_sample, ln_in_g, ln_in_b)
    lat_p, rope_p, pool_p, lat_s, rope_s, pool_s = [], [], [], [], [], []
    for i in range(DEPTH):
        w_i = (w_in[i], pool_w[i], pool_scale[i], pool_up[i], q_norm_g[i], w_uq[i], kv_norm_g[i],
               w_uk[i], w_uv[i], attn_up[i], w_out[i], ple_proj[i], ple_gate[i], ln_g[i], ln_b[i])
        hp, lat, kr, pb = trunk_layer(hp, p_prompt[i], pos_p, None, None, *w_i)
        lat_p.append(lat)
        rope_p.append(kr)
        pool_p.append(pb)
        past_lat = cache_kv_latent[i][page_table].reshape(n_seq, -1, KV_LORA)
        past_rope = cache_k_rope[i][page_table].reshape(n_seq, -1, QK_ROPE)
        hs, lat, kr, pb = trunk_layer(hs, p_sample[i], pos_s, state_pool[i], (past_lat, past_rope), *w_i)
        lat_s.append(lat)
        rope_s.append(kr)
        pool_s.append(pb)
    return (hp, hs, jnp.stack(lat_p), jnp.stack(rope_p), jnp.stack(pool_p),
            jnp.stack(lat_s), jnp.stack(rope_s), jnp.stack(pool_s))
```

```python
import functools
import math

import jax
import jax.numpy as jnp
from jax import lax
from jax.experimental import pallas as pl
from jax.experimental.pallas import tpu as pltpu

F32 = jnp.float32
BF16 = jnp.bfloat16

D_MODEL = 1024
DEPTH = 2
PAGE_SIZE = 128
POOL_WINDOWS = (2, 4, 8, 16)
POOL_WIDTH = 512
POOL_GC = 128
POOL_WMAX = 16
POOL_BUF = POOL_WMAX - 1
N_HEADS = 8
QK_NOPE = 64
QK_ROPE = 32
ROPE_HALF = QK_ROPE // 2
QK_HEAD = QK_NOPE + QK_ROPE
V_DIM = 64
Q_LORA = 384
KV_LORA = 256
ATTN_WIDTH = N_HEADS * V_DIM
ROPE_THETA = 10000.0
PLE_DIM = 256
DN_ALPHA = (2 * DEPTH) ** 0.25
NORM_EPS = 1e-6
HEAD_PAD = 128
QK_PAD = N_HEADS * HEAD_PAD
Q_SCALE = (QK_HEAD ** -0.5) * math.log2(math.e)

O_PV, O_PG, O_CQ, O_CKV, O_KR, O_AG, O_MGP, O_MGA, O_END = 0, 512, 1024, 1408, 1664, 1792, 2304, 3328, 4352

VMEM_LIMIT = 56 * 1024 * 1024


def _const_spec(shape):
    nd = len(shape)
    return pl.BlockSpec(shape, lambda *_: (0,) * nd, pipeline_mode=pl.Buffered(1))


def _layer_norm(r, g, b):
    mu = jnp.mean(r, -1, keepdims=True)
    d = r - mu
    var = jnp.mean(d * d, -1, keepdims=True)
    return d * lax.rsqrt(var + NORM_EPS) * g + b


def _rms_norm(x, g):
    return x * lax.rsqrt(jnp.mean(x * x, -1, keepdims=True) + NORM_EPS) * g


def _silu(x):
    return x * jax.nn.sigmoid(x)


def _dot(a, b):
    return jnp.dot(a, b, preferred_element_type=F32)


def _dot_nt(a, b):
    return lax.dot_general(a, b, (((1,), (1,)), ((), ())), preferred_element_type=F32)


def _rope_block(blk, c, sa, sb):
    return blk * c + pltpu.roll(blk, HEAD_PAD - ROPE_HALF, 1) * sa + pltpu.roll(blk, ROPE_HALF, 1) * sb


def _ln_kernel(x_ref, g_ref, b_ref, o_ref):
    o_ref[...] = _layer_norm(x_ref[...], g_ref[...], b_ref[...])


def _ln_call(x2d, g, b, tm):
    n = x2d.shape[0]
    return pl.pallas_call(
        _ln_kernel,
        grid=(n // tm,),
        in_specs=[pl.BlockSpec((tm, D_MODEL), lambda i: (i, 0)),
                  pl.BlockSpec((1, D_MODEL), lambda i: (0, 0)),
                  pl.BlockSpec((1, D_MODEL), lambda i: (0, 0))],
        out_specs=pl.BlockSpec((tm, D_MODEL), lambda i: (i, 0)),
        out_shape=jax.ShapeDtypeStruct((n, D_MODEL), F32),
        name="input_layer_norm",
    )(x2d, g.reshape(1, D_MODEL), b.reshape(1, D_MODEL))


def _pool_tail(pool_mean_minus_tok, pg, mgp, poolw_ref, pscale_ref, poolup_ref):
    ys = [_dot(m.astype(BF16), poolw_ref[g]) for g, m in enumerate(pool_mean_minus_tok)]
    y = jnp.concatenate(ys, axis=-1)
    yb = (y * pscale_ref[...] * _silu(pg)).astype(BF16)
    return jax.nn.sigmoid(mgp) * _dot(yb, poolup_ref[...])


def _q_heads(cq, qg_ref, wuq_ref, c, sa, sb):
    qf = _dot(_rms_norm(cq, qg_ref[...]).astype(BF16), wuq_ref[...])
    return [(_rope_block(qf[:, h * HEAD_PAD:(h + 1) * HEAD_PAD], c, sa, sb) * Q_SCALE).astype(BF16)
            for h in range(N_HEADS)]


def _rope_key(kr, c, sa, sb):
    return _rope_block(kr, c, sa, sb)


def _pre_prompt_kernel(x_ref, c_ref, sa_ref, sb_ref, w1_ref, poolw_ref, pscale_ref, poolup_ref, qg_ref,
                       wuq_ref, kvg_ref, wuk_ref, wuv_ref, vones_ref,
                       q_ref, k_ref, v_ref, lat_ref, krope_ref, gp_ref, sga_ref, ga_ref, pstate_ref,
                       ext_sc, *, tm):
    si = pl.program_id(1)
    xb = x_ref[0].astype(BF16)

    def seg(a, b):
        return _dot(xb, w1_ref[:, a:b])

    pv = seg(O_PV, O_PG)

    @pl.when(si == 0)
    def _():
        ext_sc[0:POOL_WMAX, :] = jnp.zeros((POOL_WMAX, POOL_WIDTH), F32)

    ext_sc[POOL_WMAX:POOL_WMAX + tm, :] = pv
    pos = si * tm + lax.broadcasted_iota(jnp.int32, (tm, 1), 0)
    means = []
    for g, w in enumerate(POOL_WINDOWS):
        c0 = g * POOL_GC
        acc = pv[:, c0:c0 + POOL_GC]
        for j in range(1, w):
            acc = acc + ext_sc[POOL_WMAX - j:POOL_WMAX - j + tm, c0:c0 + POOL_GC]
        cnt = jnp.minimum(pos + 1, w).astype(F32)
        means.append(acc / cnt - pv[:, c0:c0 + POOL_GC])
    tail = ext_sc[tm:tm + POOL_WMAX, :]
    pstate_ref[0] = tail
    ext_sc[0:POOL_WMAX, :] = tail
    gp_ref[0] = _pool_tail(means, seg(O_PG, O_CQ), seg(O_MGP, O_MGA), poolw_ref, pscale_ref,
                           poolup_ref).astype(BF16)
    sga_ref[0] = jax.nn.sigmoid(seg(O_MGA, O_END)).astype(BF16)
    ga_ref[0] = _silu(seg(O_AG, O_MGP)).astype(BF16)

    c, sa, sb = c_ref[...], sa_ref[...], sb_ref[...]
    for h, qh in enumerate(_q_heads(seg(O_CQ, O_CKV), qg_ref, wuq_ref, c, sa, sb)):
        q_ref[0, :, h * HEAD_PAD:(h + 1) * HEAD_PAD] = qh

    lat = _rms_norm(seg(O_CKV, O_KR), kvg_ref[...])
    lat_ref[0] = lat
    latb = lat.astype(BF16)
    krr = _rope_key(seg(O_KR, O_AG), c, sa, sb)
    krope_ref[0] = krr[:, HEAD_PAD - QK_ROPE:]
    lane = lax.broadcasted_iota(jnp.int32, (tm, HEAD_PAD), 1)
    krk = jnp.where(lane < QK_HEAD, krr, 0.0)
    kn = _dot(latb, wuk_ref[...])
    for h in range(N_HEADS):
        k_ref[0, :, h * HEAD_PAD:(h + 1) * HEAD_PAD] = (kn[:, h * HEAD_PAD:(h + 1) * HEAD_PAD] + krk).astype(BF16)
    v_ref[0] = (_dot(latb, wuv_ref[...]) + vones_ref[...]).astype(BF16)


def _pre_prompt_call(x, tabs, wl, tm):
    bsz, seq, _ = x.shape
    row = lambda width: pl.BlockSpec((1, tm, width), lambda b, s: (b, s, 0))
    tab = pl.BlockSpec((tm, HEAD_PAD), lambda b, s: (s, 0))
    in_specs = [row(D_MODEL), tab, tab, tab,
                _const_spec((D_MODEL, O_END)), _const_spec((4, POOL_GC, POOL_GC)), _const_spec((1, POOL_WIDTH)),
                _const_spec((POOL_WIDTH, D_MODEL)), _const_spec((1, Q_LORA)), _const_spec((Q_LORA, QK_PAD)),
                _const_spec((1, KV_LORA)), _const_spec((KV_LORA, QK_PAD)), _const_spec((KV_LORA, QK_PAD)),
                _const_spec((1, QK_PAD))]
    out_specs = [row(QK_PAD), row(QK_PAD), row(QK_PAD), row(KV_LORA), row(QK_ROPE), row(D_MODEL), row(D_MODEL),
                 row(ATTN_WIDTH), pl.BlockSpec((1, POOL_WMAX, POOL_WIDTH), lambda b, s: (b, 0, 0))]
    sds = jax.ShapeDtypeStruct
    out_shape = [sds((bsz, seq, QK_PAD), BF16), sds((bsz, seq, QK_PAD), BF16), sds((bsz, seq, QK_PAD), BF16),
                 sds((bsz, seq, KV_LORA), F32), sds((bsz, seq, QK_ROPE), F32), sds((bsz, seq, D_MODEL), BF16),
                 sds((bsz, seq, D_MODEL), BF16), sds((bsz, seq, ATTN_WIDTH), BF16),
                 sds((bsz, POOL_WMAX, POOL_WIDTH), F32)]
    return pl.pallas_call(
        functools.partial(_pre_prompt_kernel, tm=tm),
        grid=(bsz, seq // tm),
        in_specs=in_specs, out_specs=out_specs, out_shape=out_shape,
        scratch_shapes=[pltpu.VMEM((tm + POOL_WMAX, POOL_WIDTH), F32)],
        compiler_params=pltpu.CompilerParams(dimension_semantics=("arbitrary", "arbitrary"),
                                             vmem_limit_bytes=VMEM_LIMIT),
        name="prompt_pre",
    )(x, *tabs, wl["w1"], wl["pool_w"], wl["pool_scale"], wl["pool_up"], wl["q_norm_g"], wl["wuq"],
      wl["kv_norm_g"], wl["wuk"], wl["wuv"], wl["vones"])


def _attn_kernel(q_ref, k_ref, v_ref, o_ref, m_sc, acc_sc, *, bq):
    qi = pl.program_id(1)
    ki = pl.program_id(2)

    @pl.when(ki == 0)
    def _():
        m_sc[...] = jnp.full(m_sc.shape, -jnp.inf, F32)
        acc_sc[...] = jnp.zeros(acc_sc.shape, F32)

    def step(masked):
        if masked:
            rowi = lax.broadcasted_iota(jnp.int32, (bq, bq), 0)
            coli = lax.broadcasted_iota(jnp.int32, (bq, bq), 1)
            keep = coli <= rowi
        for h in range(N_HEADS):
            sl = slice(h * HEAD_PAD, (h + 1) * HEAD_PAD)
            s = _dot_nt(q_ref[0, :, sl], k_ref[0, :, sl])
            if masked:
                s = jnp.where(keep, s, -jnp.inf)
            m_prev = m_sc[h]
            m_new = jnp.maximum(m_prev, jnp.max(s, axis=1, keepdims=True))
            alpha = jnp.exp2(m_prev - m_new)
            p = jnp.exp2(s - m_new).astype(BF16)
            acc_sc[h] = alpha * acc_sc[h] + _dot(p, v_ref[0, :, sl])
            m_sc[h] = m_new

    @pl.when(ki < qi)
    def _():
        step(False)

    @pl.when(ki == qi)
    def _():
        step(True)
        lane = lax.broadcasted_iota(jnp.int32, (bq, HEAD_PAD), 1)
        outs = []
        for h in range(N_HEADS):
            acc = acc_sc[h]
            inv_l = jnp.where(lane >= V_DIM, 1.0 / acc, 0.0)
            outs.append(acc * pltpu.roll(inv_l, V_DIM, 1))
        for hp in range(N_HEADS // 2):
            pair = outs[2 * hp] + pltpu.roll(outs[2 * hp + 1], V_DIM, 1)
            o_ref[0, :, hp * HEAD_PAD:(hp + 1) * HEAD_PAD] = pair.astype(BF16)


def _attn_call(q, k, v, bq):
    bsz, seq, _ = q.shape
    nq = seq // bq
    qspec = pl.BlockSpec((1, bq, QK_PAD), lambda b, i, j: (b, i, 0))
    kvspec = pl.BlockSpec((1, bq, QK_PAD), lambda b, i, j: (b, jnp.minimum(i, j), 0))
    return pl.pallas_call(
        functools.partial(_attn_kernel, bq=bq),
        grid=(bsz, nq, nq),
        in_specs=[qspec, kvspec, kvspec],
        out_specs=pl.BlockSpec((1, bq, ATTN_WIDTH), lambda b, i, j: (b, i, 0)),
        out_shape=jax.ShapeDtypeStruct((bsz, seq, ATTN_WIDTH), BF16),
        scratch_shapes=[pltpu.VMEM((N_HEADS, bq, 1), F32), pltpu.VMEM((N_HEADS, bq, HEAD_PAD), F32)],
        compiler_params=pltpu.CompilerParams(dimension_semantics=("arbitrary", "arbitrary", "arbitrary"),
                                             vmem_limit_bytes=VMEM_LIMIT),
        name="prompt_attention",
    )(q, k, v)


def _post_kernel(o_ref, ga_ref, gp_ref, sga_ref, x_ref, p_ref, attnup_ref, wout_ref, pgate_ref, pproj_ref,
                 lng_ref, lnb_ref, y_ref):
    og = (o_ref[...].astype(F32) * ga_ref[...].astype(F32)).astype(BF16)
    b_attn = _dot(og, attnup_ref[...])
    mix = gp_ref[...].astype(F32) + sga_ref[...].astype(F32) * b_attn
    sub = _dot(mix.astype(BF16), wout_ref[...])
    x = x_ref[...]
    ple = jax.nn.sigmoid(_dot(x.astype(BF16), pgate_ref[...])) * _dot(p_ref[...].astype(BF16), pproj_ref[...])
    y_ref[...] = _layer_norm(DN_ALPHA * x + sub + ple, lng_ref[...], lnb_ref[...])


def _post_call(o, ga, gp, sga, x, p, wl, tm):
    n = x.shape[0]
    row = lambda width: pl.BlockSpec((tm, width), lambda i: (i, 0))
    return pl.pallas_call(
        _post_kernel,
        grid=(n // tm,),
        in_specs=[row(ATTN_WIDTH), row(ATTN_WIDTH), row(D_MODEL), row(D_MODEL), row(D_MODEL), row(PLE_DIM),
                  _const_spec((ATTN_WIDTH, D_MODEL)), _const_spec((D_MODEL, D_MODEL)),
                  _const_spec((D_MODEL, D_MODEL)), _const_spec((PLE_DIM, D_MODEL)),
                  _const_spec((1, D_MODEL)), _const_spec((1, D_MODEL))],
        out_specs=row(D_MODEL),
        out_shape=jax.ShapeDtypeStruct((n, D_MODEL), F32),
        compiler_params=pltpu.CompilerParams(dimension_semantics=("arbitrary",), vmem_limit_bytes=VMEM_LIMIT),
        name="layer_post",
    )(o, ga, gp, sga, x, p, wl["attn_up"], wl["w_out"], wl["ple_gate"], wl["ple_proj"], wl["ln_g"], wl["ln_b"])


def _pre_sample_kernel(x_ref, state_ref, c_ref, sa_ref, sb_ref, w1_ref, poolw_ref, pscale_ref, poolup_ref,
                       qg_ref, wuq_ref, kvg_ref, wukt_ref,
                       q_ref, qlat_ref, lat_ref, krope_ref, gp_ref, sga_ref, ga_ref, pv_ref):
    xb = x_ref[...].astype(BF16)

    def seg(a, b):
        return _dot(xb, w1_ref[:, a:b])

    pv = seg(O_PV, O_PG)
    pv_ref[...] = pv
    means = []
    for g, w in enumerate(POOL_WINDOWS):
        c0 = g * POOL_GC
        acc = pv[:, c0:c0 + POOL_GC]
        for j in range(1, w):
            acc = acc + state_ref[:, POOL_BUF - j, c0:c0 + POOL_GC]
        means.append(acc / float(w) - pv[:, c0:c0 + POOL_GC])
    gp_ref[...] = _pool_tail(means, seg(O_PG, O_CQ), seg(O_MGP, O_MGA), poolw_ref, pscale_ref,
                             poolup_ref).astype(BF16)
    sga_ref[...] = jax.nn.sigmoid(seg(O_MGA, O_END)).astype(BF16)
    ga_ref[...] = _silu(seg(O_AG, O_MGP)).astype(BF16)

    c, sa, sb = c_ref[...], sa_ref[...], sb_ref[...]
    qh = _q_heads(seg(O_CQ, O_CKV), qg_ref, wuq_ref, c, sa, sb)
    qf = jnp.concatenate(qh, axis=-1)
    q_ref[...] = qf
    qlat_ref[...] = _dot(qf, wukt_ref[...]).astype(BF16)
    lat_ref[...] = _rms_norm(seg(O_CKV, O_KR), kvg_ref[...])
    krope_ref[...] = _rope_key(seg(O_KR, O_AG), c, sa, sb)[:, HEAD_PAD - QK_ROPE:]


def _pre_sample_call(x, state, tabs, wl):
    n = x.shape[0]
    sds = jax.ShapeDtypeStruct
    out_shape = [sds((n, QK_PAD), BF16), sds((n, N_HEADS * KV_LORA), BF16), sds((n, KV_LORA), F32),
                 sds((n, QK_ROPE), F32), sds((n, D_MODEL), BF16), sds((n, D_MODEL), BF16),
                 sds((n, ATTN_WIDTH), BF16), sds((n, POOL_WIDTH), F32)]
    return pl.pallas_call(
        _pre_sample_kernel,
        out_shape=out_shape,
        compiler_params=pltpu.CompilerParams(vmem_limit_bytes=VMEM_LIMIT),
        name="sample_pre",
    )(x, state, *tabs, wl["w1"], wl["pool_w"], wl["pool_scale"], wl["pool_up"], wl["q_norm_g"], wl["wuq"],
      wl["kv_norm_g"], wl["wukt_bd"])


DEC_CHUNK = 1024


def _decode_kernel(pt_ref, qlat_ref, qrope_ref, latn_ref, ropen_ref, clat_hbm, crope_hbm, o_ref,
                   latbuf, ropebuf, latb16, s_sc, sem, *, layer, n_pages):
    b = pl.program_id(0)
    nb = pl.num_programs(0)
    slot = b % 2
    past = n_pages * PAGE_SIZE

    def page_copies(bb, sl, p):
        page = pt_ref[bb, p]
        rows = pl.ds(p * PAGE_SIZE, PAGE_SIZE)
        return (pltpu.make_async_copy(clat_hbm.at[layer, page], latbuf.at[sl, rows], sem.at[0, sl]),
                pltpu.make_async_copy(crope_hbm.at[layer, page], ropebuf.at[sl, rows], sem.at[1, sl]))

    def start_all(bb, sl):
        def body(p, carry):
            for cp in page_copies(bb, sl, p):
                cp.start()
            return carry
        lax.fori_loop(0, n_pages, body, 0)

    @pl.when(b == 0)
    def _():
        start_all(0, 0)

    @pl.when(b + 1 < nb)
    def _():
        start_all(b + 1, 1 - slot)

    def wait_body(p, carry):
        for cp in page_copies(b, slot, p):
            cp.wait()
        return carry
    lax.fori_loop(0, n_pages, wait_body, 0)

    ql = qlat_ref[0]
    qr = qrope_ref[0]
    for c in range(past // DEC_CHUNK):
        rows = slice(c * DEC_CHUNK, (c + 1) * DEC_CHUNK)
        lc = latbuf[slot, rows, :].astype(BF16)
        latb16[rows, :] = lc
        rc = ropebuf[slot, rows, :].astype(BF16)
        s_sc[:, rows] = _dot_nt(ql, lc) + _dot_nt(qr, rc)
    s = s_sc[...]
    latn = latn_ref[0]
    ropen = ropen_ref[0]
    s_new = (jnp.sum(ql.astype(F32) * latn, -1, keepdims=True)
             + jnp.sum(qr.astype(F32) * ropen, -1, keepdims=True))
    m = jnp.maximum(jnp.max(s, -1, keepdims=True), s_new)
    p = jnp.exp2(s - m)
    p_new = jnp.exp2(s_new - m)
    denom = jnp.sum(p, -1, keepdims=True) + p_new
    pb = p.astype(BF16)
    acc = p_new * latn
    for c in range(past // DEC_CHUNK):
        rows = slice(c * DEC_CHUNK, (c + 1) * DEC_CHUNK)
        acc = acc + _dot(pb[:, rows], latb16[rows, :])
    o_ref[0] = acc / denom


def _decode_call(page_table, qlat, qrope, lat_new, rope_new, cache_lat, cache_rope, layer):
    bsz, n_pages = page_table.shape
    past = n_pages * PAGE_SIZE
    blk = lambda s1, s2: pl.BlockSpec((1, s1, s2), lambda b, pt: (b, 0, 0))
    grid_spec = pltpu.PrefetchScalarGridSpec(
        num_scalar_prefetch=1,
        grid=(bsz,),
        in_specs=[blk(N_HEADS, KV_LORA), blk(N_HEADS, QK_ROPE), blk(1, KV_LORA), blk(1, QK_ROPE),
                  pl.BlockSpec(memory_space=pl.ANY), pl.BlockSpec(memory_space=pl.ANY)],
        out_specs=blk(N_HEADS, KV_LORA),
        scratch_shapes=[pltpu.VMEM((2, past, KV_LORA), F32), pltpu.VMEM((2, past, QK_ROPE), F32),
                        pltpu.VMEM((past, KV_LORA), BF16), pltpu.VMEM((N_HEADS, past), F32),
                        pltpu.SemaphoreType.DMA((2, 2))],
    )
    return pl.pallas_call(
        functools.partial(_decode_kernel, layer=layer, n_pages=n_pages),
        grid_spec=grid_spec,
        out_shape=jax.ShapeDtypeStruct((bsz, N_HEADS, KV_LORA), F32),
        compiler_params=pltpu.CompilerParams(dimension_semantics=("arbitrary",), vmem_limit_bytes=VMEM_LIMIT),
        name="sample_decode_attention",
    )(page_table, qlat, qrope, lat_new, rope_new, cache_lat, cache_rope)


def _olat_kernel(olat_ref, wuv_ref, o_ref):
    o_ref[...] = _dot(olat_ref[...].astype(BF16), wuv_ref[...]).astype(BF16)


def _olat_call(olat_flat, wuv_bd):
    n = olat_flat.shape[0]
    return pl.pallas_call(
        _olat_kernel,
        out_shape=jax.ShapeDtypeStruct((n, ATTN_WIDTH), BF16),
        name="sample_value_up",
    )(olat_flat, wuv_bd)


def _rope_tables(pos):
    freqs = ROPE_THETA ** (-jnp.arange(ROPE_HALF, dtype=F32) / ROPE_HALF)
    ang = pos.astype(F32)[:, None] * freqs[None, :]
    cos, sin = jnp.cos(ang), jnp.sin(ang)
    one = jnp.ones((pos.shape[0], QK_NOPE), F32)
    z64 = jnp.zeros((pos.shape[0], QK_NOPE), F32)
    z16 = jnp.zeros((pos.shape[0], ROPE_HALF), F32)
    c = jnp.concatenate([one, cos, cos, cos, cos], -1)
    sa = jnp.concatenate([z64, -sin, z16, -sin, z16], -1)
    sb = jnp.concatenate([z64, z16, sin, z16, sin], -1)
    return c, sa, sb


def _prep_layer(i, w_in, pool_w, pool_scale, pool_up, q_norm_g, w_uq, kv_norm_g, w_uk, w_uv, attn_up, w_out,
                ple_proj, ple_gate, ln_g, ln_b):
    w = w_in[i]
    wkr = w[:, 1664:1696]
    w1 = jnp.concatenate([w[:, :1664], jnp.zeros((D_MODEL, QK_NOPE), F32), wkr, wkr, w[:, 1696:]], axis=1)
    pad_head = lambda a: jnp.pad(a, ((0, 0), (0, 0), (0, HEAD_PAD - a.shape[-1]))).reshape(a.shape[0], QK_PAD)
    eye = jnp.eye(N_HEADS, dtype=F32)
    wukt = jnp.pad(jnp.transpose(w_uk[i], (1, 2, 0)), ((0, 0), (0, HEAD_PAD - QK_NOPE), (0, 0)))
    wukt_bd = (wukt[:, :, None, :] * eye[:, None, :, None]).reshape(QK_PAD, N_HEADS * KV_LORA)
    wuv_bd = (jnp.transpose(w_uv[i], (1, 0, 2))[:, :, None, :] * eye[:, None, :, None]).reshape(
        N_HEADS * KV_LORA, ATTN_WIDTH)
    vones = jnp.tile(jnp.concatenate([jnp.zeros((V_DIM,), F32), jnp.ones((HEAD_PAD - V_DIM,), F32)]),
                     N_HEADS).reshape(1, QK_PAD)
    return {
        "w1": w1.astype(BF16),
        "pool_w": pool_w[i].astype(BF16),
        "pool_scale": pool_scale[i].reshape(1, POOL_WIDTH),
        "pool_up": pool_up[i].astype(BF16),
        "q_norm_g": q_norm_g[i].reshape(1, Q_LORA),
        "wuq": pad_head(w_uq[i].reshape(Q_LORA, N_HEADS, QK_HEAD)).astype(BF16),
        "kv_norm_g": kv_norm_g[i].reshape(1, KV_LORA),
        "wuk": pad_head(w_uk[i]).astype(BF16),
        "wuv": pad_head(w_uv[i]).astype(BF16),
        "vones": vones,
        "wukt_bd": wukt_bd.astype(BF16),
        "wuv_bd": wuv_bd.astype(BF16),
        "attn_up": attn_up[i].astype(BF16),
        "w_out": w_out[i].astype(BF16),
        "ple_gate": ple_gate[i].astype(BF16),
        "ple_proj": ple_proj[i].astype(BF16),
        "ln_g": ln_g[i].reshape(1, D_MODEL),
        "ln_b": ln_b[i].reshape(1, D_MODEL),
    }


def kernel(x_prompt, x_sample, cache_kv_latent, cache_k_rope, state_pool, page_table, p_prompt, p_sample,
           ln_in_g, ln_in_b, w_in, pool_w, pool_scale, pool_up, q_norm_g, w_uq, kv_norm_g, w_uk, w_uv,
           attn_up, w_out, ple_proj, ple_gate, ln_g, ln_b):
    bsz, seq, _ = x_prompt.shape
    nsmp = x_sample.shape[0]
    n_pages = page_table.shape[1]
    past_len = n_pages * PAGE_SIZE
    n_tok = bsz * seq
    tm_pre, tm_post, bq = 512, 512, 512

    tabs_p = _rope_tables(jnp.arange(seq))
    tabs_s = _rope_tables(jnp.full((1,), past_len))

    hp = _ln_call(x_prompt.reshape(n_tok, D_MODEL), ln_in_g, ln_in_b, 512)
    hs = _ln_call(x_sample.reshape(nsmp, D_MODEL), ln_in_g, ln_in_b, nsmp)

    lat_p, rope_p, pool_p, lat_s, rope_s, pool_s = [], [], [], [], [], []
    for i in range(DEPTH):
        wl = _prep_layer(i, w_in, pool_w, pool_scale, pool_up, q_norm_g, w_uq, kv_norm_g, w_uk, w_uv, attn_up,
                         w_out, ple_proj, ple_gate, ln_g, ln_b)
        q, k, v, lat, krope, gp, sga, ga, pstate = _pre_prompt_call(hp.reshape(bsz, seq, D_MODEL), tabs_p, wl, tm_pre)
        o = _attn_call(q, k, v, bq)
        hp = _post_call(o.reshape(n_tok, ATTN_WIDTH), ga.reshape(n_tok, ATTN_WIDTH), gp.reshape(n_tok, D_MODEL),
                        sga.reshape(n_tok, D_MODEL), hp, p_prompt[i].reshape(n_tok, PLE_DIM), wl, tm_post)
        lat_p.append(lat)
        rope_p.append(krope)
        pool_p.append(pstate[:, 1:, :])
        qs, qlat, lat_n, rope_n, gp_s, sga_s, ga_s, pv_s = _pre_sample_call(hs, state_pool[i], tabs_s, wl)
        qrope = qs.reshape(nsmp, N_HEADS, HEAD_PAD)[:, :, QK_NOPE:QK_HEAD]
        olat = _decode_call(page_table, qlat.reshape(nsmp, N_HEADS, KV_LORA), qrope,
                            lat_n.reshape(nsmp, 1, KV_LORA), rope_n.reshape(nsmp, 1, QK_ROPE),
                            cache_kv_latent, cache_k_rope, i)
        o_s = _olat_call(olat.reshape(nsmp, N_HEADS * KV_LORA), wl["wuv_bd"])
        hs = _post_call(o_s, ga_s, gp_s, sga_s, hs, p_sample[i].reshape(nsmp, PLE_DIM), wl, nsmp)
        lat_s.append(lat_n.reshape(nsmp, 1, KV_LORA))
        rope_s.append(rope_n.reshape(nsmp, 1, QK_ROPE))
        pool_s.append(jnp.concatenate([state_pool[i][:, 1:, :], pv_s[:, None, :]], axis=1))

    return (hp.reshape(bsz, seq, D_MODEL), hs.reshape(nsmp, 1, D_MODEL), jnp.stack(lat_p), jnp.stack(rope_p),
            jnp.stack(pool_p), jnp.stack(lat_s), jnp.stack(rope_s), jnp.stack(pool_s))
```

```python
import functools
import math

import jax
import jax.numpy as jnp
from jax import lax
from jax.experimental import pallas as pl
from jax.experimental.pallas import tpu as pltpu

F32 = jnp.float32
BF16 = jnp.bfloat16

D_MODEL = 1024
DEPTH = 2
PAGE_SIZE = 128
POOL_WINDOWS = (2, 4, 8, 16)
POOL_WIDTH = 512
POOL_GC = 128
POOL_WMAX = 16
POOL_BUF = POOL_WMAX - 1
N_HEADS = 8
QK_NOPE = 64
QK_ROPE = 32
ROPE_HALF = QK_ROPE // 2
QK_HEAD = QK_NOPE + QK_ROPE
V_DIM = 64
Q_LORA = 384
KV_LORA = 256
ATTN_WIDTH = N_HEADS * V_DIM
ROPE_THETA = 10000.0
PLE_DIM = 256
DN_ALPHA = (2 * DEPTH) ** 0.25
NORM_EPS = 1e-6
HEAD_PAD = 128
QK_PAD = N_HEADS * HEAD_PAD
Q_SCALE = (QK_HEAD ** -0.5) * math.log2(math.e)

O_PV, O_PG, O_CQ, O_CKV, O_KR, O_AG, O_MGP, O_MGA, O_END = 0, 512, 1024, 1408, 1664, 1792, 2304, 3328, 4352

VMEM_LIMIT = 56 * 1024 * 1024


def _const_spec(shape):
    nd = len(shape)
    return pl.BlockSpec(shape, lambda *_: (0,) * nd, pipeline_mode=pl.Buffered(1))


def _layer_norm(r, g, b):
    mu = jnp.mean(r, -1, keepdims=True)
    d = r - mu
    var = jnp.mean(d * d, -1, keepdims=True)
    return d * lax.rsqrt(var + NORM_EPS) * g + b


def _rms_norm(x, g):
    return x * lax.rsqrt(jnp.mean(x * x, -1, keepdims=True) + NORM_EPS) * g


def _silu(x):
    return x * jax.nn.sigmoid(x)


def _dot(a, b):
    return jnp.dot(a, b, preferred_element_type=F32)


def _dot_nt(a, b):
    return lax.dot_general(a, b, (((1,), (1,)), ((), ())), preferred_element_type=F32)


def _rope_block(blk, c, sa, sb):
    return blk * c + pltpu.roll(blk, HEAD_PAD - ROPE_HALF, 1) * sa + pltpu.roll(blk, ROPE_HALF, 1) * sb


def _pool_tail(pool_mean_minus_tok, pg, mgp, poolw_ref, pscale_ref, poolup_ref):
    ys = [_dot(m.astype(BF16), poolw_ref[g]) for g, m in enumerate(pool_mean_minus_tok)]
    y = jnp.concatenate(ys, axis=-1)
    yb = (y * pscale_ref[...] * _silu(pg)).astype(BF16)
    return jax.nn.sigmoid(mgp) * _dot(yb, poolup_ref[...])


def _q_heads(cq, qg_ref, wuq_ref, c, sa, sb):
    qf = _dot(_rms_norm(cq, qg_ref[...]).astype(BF16), wuq_ref[...])
    return [(_rope_block(qf[:, h * HEAD_PAD:(h + 1) * HEAD_PAD], c, sa, sb) * Q_SCALE).astype(BF16)
            for h in range(N_HEADS)]


def _rope_key(kr, c, sa, sb):
    return _rope_block(kr, c, sa, sb)


def _pre_prompt_kernel(x_ref, lnig_ref, lnib_ref, c_ref, sa_ref, sb_ref, w1_ref, poolw_ref, pscale_ref, poolup_ref,
                       qg_ref, wuq_ref, kvg_ref, wuk_ref, wuvt_ref,
                       q_ref, k_ref, vt_ref, lat_ref, krope_ref, gp_ref, sga_ref, ga_ref, pstate_ref,
                       ext_sc, *, tm, input_ln):
    si = pl.program_id(1)
    x = x_ref[0]
    if input_ln:
        x = _layer_norm(x, lnig_ref[...], lnib_ref[...])
    xb = x.astype(BF16)

    def seg(a, b):
        return _dot(xb, w1_ref[:, a:b])

    pv = seg(O_PV, O_PG)

    @pl.when(si == 0)
    def _():
        ext_sc[0:POOL_WMAX, :] = jnp.zeros((POOL_WMAX, POOL_WIDTH), F32)

    ext_sc[POOL_WMAX:POOL_WMAX + tm, :] = pv
    pos = si * tm + lax.broadcasted_iota(jnp.int32, (tm, 1), 0)
    means = []
    for g, w in enumerate(POOL_WINDOWS):
        c0 = g * POOL_GC
        acc = pv[:, c0:c0 + POOL_GC]
        for j in range(1, w):
            acc = acc + ext_sc[POOL_WMAX - j:POOL_WMAX - j + tm, c0:c0 + POOL_GC]
        cnt = jnp.minimum(pos + 1, w).astype(F32)
        means.append(acc / cnt - pv[:, c0:c0 + POOL_GC])
    tail = ext_sc[tm:tm + POOL_WMAX, :]
    pstate_ref[0] = tail
    ext_sc[0:POOL_WMAX, :] = tail
    gp_ref[0] = _pool_tail(means, seg(O_PG, O_CQ), seg(O_MGP, O_MGA), poolw_ref, pscale_ref,
                           poolup_ref).astype(BF16)
    sga_ref[0] = jax.nn.sigmoid(seg(O_MGA, O_END)).astype(BF16)
    ga_ref[0] = _silu(seg(O_AG, O_MGP)).astype(BF16)

    c, sa, sb = c_ref[...], sa_ref[...], sb_ref[...]
    for h, qh in enumerate(_q_heads(seg(O_CQ, O_CKV), qg_ref, wuq_ref, c, sa, sb)):
        q_ref[0, :, h * HEAD_PAD:(h + 1) * HEAD_PAD] = qh

    lat = _rms_norm(seg(O_CKV, O_KR), kvg_ref[...])
    lat_ref[0] = lat
    latb = lat.astype(BF16)
    krr = _rope_key(seg(O_KR, O_AG), c, sa, sb)
    krope_ref[0] = krr[:, HEAD_PAD - QK_ROPE:]
    lane = lax.broadcasted_iota(jnp.int32, (tm, HEAD_PAD), 1)
    krk = jnp.where(lane < QK_HEAD, krr, 0.0)
    kn = _dot(latb, wuk_ref[...])
    for h in range(N_HEADS):
        k_ref[0, :, h * HEAD_PAD:(h + 1) * HEAD_PAD] = (kn[:, h * HEAD_PAD:(h + 1) * HEAD_PAD] + krk).astype(BF16)
    vt = _dot_nt(wuvt_ref[...], latb)
    rowi = lax.broadcasted_iota(jnp.int32, (QK_PAD, tm), 0)
    vt_ref[0] = jnp.where((rowi & (HEAD_PAD - 1)) >= V_DIM, 1.0, vt).astype(BF16)


def _pre_prompt_call(x, ln_in, tabs, wl, tm, input_ln):
    bsz, seq, _ = x.shape
    row = lambda width: pl.BlockSpec((1, tm, width), lambda b, s: (b, s, 0))
    tab = pl.BlockSpec((tm, HEAD_PAD), lambda b, s: (s, 0))
    in_specs = [row(D_MODEL), _const_spec((1, D_MODEL)), _const_spec((1, D_MODEL)), tab, tab, tab,
                _const_spec((D_MODEL, O_END)), _const_spec((4, POOL_GC, POOL_GC)), _const_spec((1, POOL_WIDTH)),
                _const_spec((POOL_WIDTH, D_MODEL)), _const_spec((1, Q_LORA)), _const_spec((Q_LORA, QK_PAD)),
                _const_spec((1, KV_LORA)), _const_spec((KV_LORA, QK_PAD)), _const_spec((QK_PAD, KV_LORA))]
    out_specs = [row(QK_PAD), row(QK_PAD), pl.BlockSpec((1, QK_PAD, tm), lambda b, s: (b, 0, s)),
                 row(KV_LORA), row(QK_ROPE), row(D_MODEL), row(D_MODEL),
                 row(ATTN_WIDTH), pl.BlockSpec((1, POOL_WMAX, POOL_WIDTH), lambda b, s: (b, 0, 0))]
    sds = jax.ShapeDtypeStruct
    out_shape = [sds((bsz, seq, QK_PAD), BF16), sds((bsz, seq, QK_PAD), BF16), sds((bsz, QK_PAD, seq), BF16),
                 sds((bsz, seq, KV_LORA), F32), sds((bsz, seq, QK_ROPE), F32), sds((bsz, seq, D_MODEL), BF16),
                 sds((bsz, seq, D_MODEL), BF16), sds((bsz, seq, ATTN_WIDTH), BF16),
                 sds((bsz, POOL_WMAX, POOL_WIDTH), F32)]
    return pl.pallas_call(
        functools.partial(_pre_prompt_kernel, tm=tm, input_ln=input_ln),
        grid=(bsz, seq // tm),
        in_specs=in_specs, out_specs=out_specs, out_shape=out_shape,
        scratch_shapes=[pltpu.VMEM((tm + POOL_WMAX, POOL_WIDTH), F32)],
        compiler_params=pltpu.CompilerParams(dimension_semantics=("arbitrary", "arbitrary"),
                                             vmem_limit_bytes=VMEM_LIMIT),
        name="prompt_pre",
    )(x, *ln_in, *tabs, wl["w1"], wl["pool_w"], wl["pool_scale"], wl["pool_up"], wl["q_norm_g"], wl["wuq"],
      wl["kv_norm_g"], wl["wuk"], wl["wuvt"])


def _attn_kernel(q_ref, k_ref, vt_ref, o_ref, m_sc, acc_sc, ot_sc, *, bq, bk):
    qi = pl.program_id(1)
    ki = pl.program_id(2)
    kper = bq // bk

    @pl.when(ki == 0)
    def _():
        m_sc[...] = jnp.full(m_sc.shape, -jnp.inf, F32)
        acc_sc[...] = jnp.zeros(acc_sc.shape, F32)

    def step(masked):
        if masked:
            key_pos = ki * bk + lax.broadcasted_iota(jnp.int32, (bk, bq), 0)
            qry_pos = qi * bq + lax.broadcasted_iota(jnp.int32, (bk, bq), 1)
            keep = key_pos <= qry_pos
        for h in range(N_HEADS):
            sl = slice(h * HEAD_PAD, (h + 1) * HEAD_PAD)
            st = _dot_nt(k_ref[0, :, sl], q_ref[0, :, sl])
            if masked:
                st = jnp.where(keep, st, -jnp.inf)
            m_prev = m_sc[h]
            m_new = jnp.maximum(m_prev, jnp.max(st, axis=0, keepdims=True))
            alpha = jnp.exp2(m_prev - m_new)
            pt = jnp.exp2(st - m_new).astype(BF16)
            acc_sc[h] = alpha * acc_sc[h] + _dot(vt_ref[0, sl, :], pt)
            m_sc[h] = m_new

    @pl.when(ki < qi * kper)
    def _():
        step(False)

    @pl.when(jnp.logical_and(ki >= qi * kper, ki < (qi + 1) * kper))
    def _():
        step(True)

    @pl.when(ki == (qi + 1) * kper - 1)
    def _():
        for h in range(N_HEADS):
            acc = acc_sc[h]
            ot_sc[h * V_DIM:(h + 1) * V_DIM, :] = acc[:V_DIM, :] / acc[V_DIM:V_DIM + 1, :]
        o_ref[0] = ot_sc[...].T.astype(BF16)


def _attn_call(q, k, vt, bq, bk):
    bsz, seq, _ = q.shape
    kper = bq // bk
    qspec = pl.BlockSpec((1, bq, QK_PAD), lambda b, i, j: (b, i, 0))
    kspec = pl.BlockSpec((1, bk, QK_PAD), lambda b, i, j: (b, jnp.minimum(j, (i + 1) * kper - 1), 0))
    vspec = pl.BlockSpec((1, QK_PAD, bk), lambda b, i, j: (b, 0, jnp.minimum(j, (i + 1) * kper - 1)))
    return pl.pallas_call(
        functools.partial(_attn_kernel, bq=bq, bk=bk),
        grid=(bsz, seq // bq, seq // bk),
        in_specs=[qspec, kspec, vspec],
        out_specs=pl.BlockSpec((1, bq, ATTN_WIDTH), lambda b, i, j: (b, i, 0)),
        out_shape=jax.ShapeDtypeStruct((bsz, seq, ATTN_WIDTH), BF16),
        scratch_shapes=[pltpu.VMEM((N_HEADS, 1, bq), F32), pltpu.VMEM((N_HEADS, HEAD_PAD, bq), F32),
                        pltpu.VMEM((ATTN_WIDTH, bq), F32)],
        compiler_params=pltpu.CompilerParams(dimension_semantics=("arbitrary", "arbitrary", "arbitrary"),
                                             vmem_limit_bytes=VMEM_LIMIT),
        name="prompt_attention",
    )(q, k, vt)


def _post_kernel(o_ref, ga_ref, gp_ref, sga_ref, x_ref, p_ref, lnig_ref, lnib_ref, attnup_ref, wout_ref, pgate_ref,
                 pproj_ref, lng_ref, lnb_ref, y_ref, *, input_ln):
    og = (o_ref[...].astype(F32) * ga_ref[...].astype(F32)).astype(BF16)
    b_attn = _dot(og, attnup_ref[...])
    mix = gp_ref[...].astype(F32) + sga_ref[...].astype(F32) * b_attn
    sub = _dot(mix.astype(BF16), wout_ref[...])
    x = x_ref[...]
    if input_ln:
        x = _layer_norm(x, lnig_ref[...], lnib_ref[...])
    ple = jax.nn.sigmoid(_dot(x.astype(BF16), pgate_ref[...])) * _dot(p_ref[...].astype(BF16), pproj_ref[...])
    y_ref[...] = _layer_norm(DN_ALPHA * x + sub + ple, lng_ref[...], lnb_ref[...])


def _post_call(o, ga, gp, sga, x, p, ln_in, wl, tm, input_ln):
    n = x.shape[0]
    row = lambda width: pl.BlockSpec((tm, width), lambda i: (i, 0))
    return pl.pallas_call(
        functools.partial(_post_kernel, input_ln=input_ln),
        grid=(n // tm,),
        in_specs=[row(ATTN_WIDTH), row(ATTN_WIDTH), row(D_MODEL), row(D_MODEL), row(D_MODEL), row(PLE_DIM),
                  _const_spec((1, D_MODEL)), _const_spec((1, D_MODEL)),
                  _const_spec((ATTN_WIDTH, D_MODEL)), _const_spec((D_MODEL, D_MODEL)),
                  _const_spec((D_MODEL, D_MODEL)), _const_spec((PLE_DIM, D_MODEL)),
                  _const_spec((1, D_MODEL)), _const_spec((1, D_MODEL))],
        out_specs=row(D_MODEL),
        out_shape=jax.ShapeDtypeStruct((n, D_MODEL), F32),
        compiler_params=pltpu.CompilerParams(dimension_semantics=("arbitrary",), vmem_limit_bytes=VMEM_LIMIT),
        name="layer_post",
    )(o, ga, gp, sga, x, p, *ln_in, wl["attn_up"], wl["w_out"], wl["ple_gate"], wl["ple_proj"], wl["ln_g"],
      wl["ln_b"])


def _pre_sample_kernel(x_ref, lnig_ref, lnib_ref, state_ref, c_ref, sa_ref, sb_ref, w1_ref, poolw_ref, pscale_ref,
                       poolup_ref, qg_ref, wuq_ref, kvg_ref, wukt_ref,
                       q_ref, qlat_ref, lat_ref, krope_ref, gp_ref, sga_ref, ga_ref, pv_ref, *, input_ln):
    x = x_ref[...]
    if input_ln:
        x = _layer_norm(x, lnig_ref[...], lnib_ref[...])
    xb = x.astype(BF16)

    def seg(a, b):
        return _dot(xb, w1_ref[:, a:b])

    pv = seg(O_PV, O_PG)
    pv_ref[...] = pv
    means = []
    for g, w in enumerate(POOL_WINDOWS):
        c0 = g * POOL_GC
        acc = pv[:, c0:c0 + POOL_GC]
        for j in range(1, w):
            acc = acc + state_ref[POOL_BUF - j, :, c0:c0 + POOL_GC]
        means.append(acc / float(w) - pv[:, c0:c0 + POOL_GC])
    gp_ref[...] = _pool_tail(means, seg(O_PG, O_CQ), seg(O_MGP, O_MGA), poolw_ref, pscale_ref,
                             poolup_ref).astype(BF16)
    sga_ref[...] = jax.nn.sigmoid(seg(O_MGA, O_END)).astype(BF16)
    ga_ref[...] = _silu(seg(O_AG, O_MGP)).astype(BF16)

    c, sa, sb = c_ref[...], sa_ref[...], sb_ref[...]
    qh = _q_heads(seg(O_CQ, O_CKV), qg_ref, wuq_ref, c, sa, sb)
    qf = jnp.concatenate(qh, axis=-1)
    q_ref[...] = qf
    qlat_ref[...] = _dot(qf, wukt_ref[...]).astype(BF16)
    lat_ref[...] = _rms_norm(seg(O_CKV, O_KR), kvg_ref[...])
    krope_ref[...] = _rope_key(seg(O_KR, O_AG), c, sa, sb)[:, HEAD_PAD - QK_ROPE:]


def _pre_sample_call(x, ln_in, state_t, tabs, wl, input_ln):
    n = x.shape[0]
    sds = jax.ShapeDtypeStruct
    out_shape = [sds((n, QK_PAD), BF16), sds((n, N_HEADS * KV_LORA), BF16), sds((n, KV_LORA), F32),
                 sds((n, QK_ROPE), F32), sds((n, D_MODEL), BF16), sds((n, D_MODEL), BF16),
                 sds((n, ATTN_WIDTH), BF16), sds((n, POOL_WIDTH), F32)]
    return pl.pallas_call(
        functools.partial(_pre_sample_kernel, input_ln=input_ln),
        out_shape=out_shape,
        compiler_params=pltpu.CompilerParams(vmem_limit_bytes=VMEM_LIMIT),
        name="sample_pre",
    )(x, *ln_in, state_t, *tabs, wl["w1"], wl["pool_w"], wl["pool_scale"], wl["pool_up"], wl["q_norm_g"], wl["wuq"],
      wl["kv_norm_g"], wl["wukt_bd"])


DEC_CHUNK = 1024


def _decode_kernel(pt_ref, qlat_ref, qrope_ref, latn_ref, ropen_ref, clat_hbm, cropet_hbm, o_ref,
                   latbuf, ropebuf, latb16, s_sc, sem, *, layer, n_pages):
    b = pl.program_id(0)
    nb = pl.num_programs(0)
    slot = b % 2
    past = n_pages * PAGE_SIZE

    def page_copies(sl, p, page):
        rows = pl.ds(p * PAGE_SIZE, PAGE_SIZE)
        return (pltpu.make_async_copy(clat_hbm.at[layer, page], latbuf.at[sl, rows], sem.at[0, sl]),
                pltpu.make_async_copy(cropet_hbm.at[layer, page], ropebuf.at[sl, p], sem.at[1, sl]))

    def start_all(bb, sl):
        def body(p, carry):
            for cp in page_copies(sl, p, pt_ref[bb, p]):
                cp.start()
            return carry
        lax.fori_loop(0, n_pages, body, 0)

    @pl.when(b == 0)
    def _():
        start_all(0, 0)

    @pl.when(b + 1 < nb)
    def _():
        start_all(b + 1, 1 - slot)

    for p in range(n_pages):
        for cp in page_copies(slot, p, 0):
            cp.wait()

    ql = qlat_ref[0]
    qr = qrope_ref[0]
    chunk_pages = DEC_CHUNK // PAGE_SIZE
    for c in range(past // DEC_CHUNK):
        rows = slice(c * DEC_CHUNK, (c + 1) * DEC_CHUNK)
        lc = latbuf[slot, rows, :].astype(BF16)
        latb16[rows, :] = lc
        s_rope = jnp.concatenate([_dot(qr, ropebuf[slot, c * chunk_pages + t].astype(BF16))
                                  for t in range(chunk_pages)], axis=1)
        s_sc[:, rows] = _dot_nt(ql, lc) + s_rope
    s = s_sc[...]
    latn = latn_ref[0]
    ropen = ropen_ref[0]
    s_new = (jnp.sum(ql.astype(F32) * latn, -1, keepdims=True)
             + jnp.sum(qr.astype(F32) * ropen, -1, keepdims=True))
    m = jnp.maximum(jnp.max(s, -1, keepdims=True), s_new)
    p = jnp.exp2(s - m)
    p_new = jnp.exp2(s_new - m)
    denom = jnp.sum(p, -1, keepdims=True) + p_new
    pb = p.astype(BF16)
    acc = p_new * latn
    for c in range(past // DEC_CHUNK):
        rows = slice(c * DEC_CHUNK, (c + 1) * DEC_CHUNK)
        acc = acc + _dot(pb[:, rows], latb16[rows, :])
    o_ref[0] = acc / denom


def _decode_call(page_table, qlat, qrope, lat_new, rope_new, cache_lat, cache_rope, layer):
    bsz, n_pages = page_table.shape
    past = n_pages * PAGE_SIZE
    blk = lambda s1, s2: pl.BlockSpec((1, s1, s2), lambda b, pt: (b, 0, 0))
    grid_spec = pltpu.PrefetchScalarGridSpec(
        num_scalar_prefetch=1,
        grid=(bsz,),
        in_specs=[blk(N_HEADS, KV_LORA), blk(N_HEADS, QK_ROPE), blk(1, KV_LORA), blk(1, QK_ROPE),
                  pl.BlockSpec(memory_space=pl.ANY), pl.BlockSpec(memory_space=pl.ANY)],
        out_specs=blk(N_HEADS, KV_LORA),
        scratch_shapes=[pltpu.VMEM((2, past, KV_LORA), F32), pltpu.VMEM((2, n_pages, QK_ROPE, PAGE_SIZE), F32),
                        pltpu.VMEM((past, KV_LORA), BF16), pltpu.VMEM((N_HEADS, past), F32),
                        pltpu.SemaphoreType.DMA((2, 2))],
    )
    return pl.pallas_call(
        functools.partial(_decode_kernel, layer=layer, n_pages=n_pages),
        grid_spec=grid_spec,
        out_shape=jax.ShapeDtypeStruct((bsz, N_HEADS, KV_LORA), F32),
        compiler_params=pltpu.CompilerParams(dimension_semantics=("arbitrary",), vmem_limit_bytes=VMEM_LIMIT),
        name="sample_decode_attention",
    )(page_table, qlat, qrope, lat_new, rope_new, cache_lat, cache_rope)


def _olat_kernel(olat_ref, wuv_ref, o_ref):
    o_ref[...] = _dot(olat_ref[...].astype(BF16), wuv_ref[...]).astype(BF16)


def _olat_call(olat_flat, wuv_bd):
    n = olat_flat.shape[0]
    return pl.pallas_call(
        _olat_kernel,
        out_shape=jax.ShapeDtypeStruct((n, ATTN_WIDTH), BF16),
        name="sample_value_up",
    )(olat_flat, wuv_bd)


def _rope_tables(pos):
    freqs = ROPE_THETA ** (-jnp.arange(ROPE_HALF, dtype=F32) / ROPE_HALF)
    ang = pos.astype(F32)[:, None] * freqs[None, :]
    cos, sin = jnp.cos(ang), jnp.sin(ang)
    one = jnp.ones((pos.shape[0], QK_NOPE), F32)
    z64 = jnp.zeros((pos.shape[0], QK_NOPE), F32)
    z16 = jnp.zeros((pos.shape[0], ROPE_HALF), F32)
    c = jnp.concatenate([one, cos, cos, cos, cos], -1)
    sa = jnp.concatenate([z64, -sin, z16, -sin, z16], -1)
    sb = jnp.concatenate([z64, z16, sin, z16, sin], -1)
    return c, sa, sb


def _prep_layer(i, w_in, pool_w, pool_scale, pool_up, q_norm_g, w_uq, kv_norm_g, w_uk, w_uv, attn_up, w_out,
                ple_proj, ple_gate, ln_g, ln_b):
    w = w_in[i]
    wkr = w[:, 1664:1696]
    w1 = jnp.concatenate([w[:, :1664], jnp.zeros((D_MODEL, QK_NOPE), F32), wkr, wkr, w[:, 1696:]], axis=1)
    pad_head = lambda a: jnp.pad(a, ((0, 0), (0, 0), (0, HEAD_PAD - a.shape[-1]))).reshape(a.shape[0], QK_PAD)
    eye = jnp.eye(N_HEADS, dtype=F32)
    wukt = jnp.pad(jnp.transpose(w_uk[i], (1, 2, 0)), ((0, 0), (0, HEAD_PAD - QK_NOPE), (0, 0)))
    wukt_bd = (wukt[:, :, None, :] * eye[:, None, :, None]).reshape(QK_PAD, N_HEADS * KV_LORA)
    wuv_bd = (jnp.transpose(w_uv[i], (1, 0, 2))[:, :, None, :] * eye[:, None, :, None]).reshape(
        N_HEADS * KV_LORA, ATTN_WIDTH)
    return {
        "w1": w1.astype(BF16),
        "pool_w": pool_w[i].astype(BF16),
        "pool_scale": pool_scale[i].reshape(1, POOL_WIDTH),
        "pool_up": pool_up[i].astype(BF16),
        "q_norm_g": q_norm_g[i].reshape(1, Q_LORA),
        "wuq": pad_head(w_uq[i].reshape(Q_LORA, N_HEADS, QK_HEAD)).astype(BF16),
        "kv_norm_g": kv_norm_g[i].reshape(1, KV_LORA),
        "wuk": pad_head(w_uk[i]).astype(BF16),
        "wuvt": pad_head(w_uv[i]).T.astype(BF16),
        "wukt_bd": wukt_bd.astype(BF16),
        "wuv_bd": wuv_bd.astype(BF16),
        "attn_up": attn_up[i].astype(BF16),
        "w_out": w_out[i].astype(BF16),
        "ple_gate": ple_gate[i].astype(BF16),
        "ple_proj": ple_proj[i].astype(BF16),
        "ln_g": ln_g[i].reshape(1, D_MODEL),
        "ln_b": ln_b[i].reshape(1, D_MODEL),
    }


def kernel(x_prompt, x_sample, cache_kv_latent, cache_k_rope, state_pool, page_table, p_prompt, p_sample,
           ln_in_g, ln_in_b, w_in, pool_w, pool_scale, pool_up, q_norm_g, w_uq, kv_norm_g, w_uk, w_uv,
           attn_up, w_out, ple_proj, ple_gate, ln_g, ln_b):
    bsz, seq, _ = x_prompt.shape
    nsmp = x_sample.shape[0]
    n_pages = page_table.shape[1]
    past_len = n_pages * PAGE_SIZE
    n_tok = bsz * seq
    tm_pre, tm_post, bq, bk = 512, 512, 1024, 1024

    tabs_p = _rope_tables(jnp.arange(seq))
    tabs_s = _rope_tables(jnp.full((1,), past_len))
    ln_in = (ln_in_g.reshape(1, D_MODEL), ln_in_b.reshape(1, D_MODEL))
    cache_rope_t = jnp.swapaxes(cache_k_rope, 2, 3)

    hp = x_prompt.reshape(n_tok, D_MODEL)
    hs = x_sample.reshape(nsmp, D_MODEL)

    lat_p, rope_p, pool_p, lat_s, rope_s, pool_s = [], [], [], [], [], []
    for i in range(DEPTH):
        first = i == 0
        wl = _prep_layer(i, w_in, pool_w, pool_scale, pool_up, q_norm_g, w_uq, kv_norm_g, w_uk, w_uv, attn_up,
                         w_out, ple_proj, ple_gate, ln_g, ln_b)
        q, k, vt, lat, krope, gp, sga, ga, pstate = _pre_prompt_call(hp.reshape(bsz, seq, D_MODEL), ln_in, tabs_p,
                                                                     wl, tm_pre, first)
        o = _attn_call(q, k, vt, bq, bk)
        hp = _post_call(o.reshape(n_tok, ATTN_WIDTH), ga.reshape(n_tok, ATTN_WIDTH), gp.reshape(n_tok, D_MODEL),
                        sga.reshape(n_tok, D_MODEL), hp, p_prompt[i].reshape(n_tok, PLE_DIM), ln_in, wl, tm_post,
                        first)
        lat_p.append(lat)
        rope_p.append(krope)
        pool_p.append(pstate[:, 1:, :])
        qs, qlat, lat_n, rope_n, gp_s, sga_s, ga_s, pv_s = _pre_sample_call(
            hs, ln_in, jnp.swapaxes(state_pool[i], 0, 1), tabs_s, wl, first)
        qrope = qs.reshape(nsmp, N_HEADS, HEAD_PAD)[:, :, QK_NOPE:QK_HEAD]
        olat = _decode_call(page_table, qlat.reshape(nsmp, N_HEADS, KV_LORA), qrope,
                            lat_n.reshape(nsmp, 1, KV_LORA), rope_n.reshape(nsmp, 1, QK_ROPE),
                            cache_kv_latent, cache_rope_t, i)
        o_s = _olat_call(olat.reshape(nsmp, N_HEADS * KV_LORA), wl["wuv_bd"])
        hs = _post_call(o_s, ga_s, gp_s, sga_s, hs, p_sample[i].reshape(nsmp, PLE_DIM), ln_in, wl, nsmp, first)
        lat_s.append(lat_n.reshape(nsmp, 1, KV_LORA))
        rope_s.append(rope_n.reshape(nsmp, 1, QK_ROPE))
        pool_s.append(jnp.concatenate([state_pool[i][:, 1:, :], pv_s[:, None, :]], axis=1))

    return (hp.reshape(bsz, seq, D_MODEL), hs.reshape(nsmp, 1, D_MODEL), jnp.stack(lat_p), jnp.stack(rope_p),
            jnp.stack(pool_p), jnp.stack(lat_s), jnp.stack(rope_s), jnp.stack(pool_s))
```

```python
import functools
import math

import jax
import jax.numpy as jnp
from jax import lax
from jax.experimental import pallas as pl
from jax.experimental.pallas import tpu as pltpu

F32 = jnp.float32
BF16 = jnp.bfloat16

D_MODEL = 1024
DEPTH = 2
PAGE_SIZE = 128
POOL_WINDOWS = (2, 4, 8, 16)
POOL_WIDTH = 512
POOL_GC = 128
POOL_WMAX = 16
POOL_BUF = POOL_WMAX - 1
N_HEADS = 8
QK_NOPE = 64
QK_ROPE = 32
ROPE_HALF = QK_ROPE // 2
QK_HEAD = QK_NOPE + QK_ROPE
V_DIM = 64
Q_LORA = 384
KV_LORA = 256
ATTN_WIDTH = N_HEADS * V_DIM
ROPE_THETA = 10000.0
PLE_DIM = 256
DN_ALPHA = (2 * DEPTH) ** 0.25
NORM_EPS = 1e-6
HEAD_PAD = 128
QK_PAD = N_HEADS * HEAD_PAD
Q_SCALE = (QK_HEAD ** -0.5) * math.log2(math.e)

O_PV, O_PG, O_CQ, O_CKV, O_KR, O_AG, O_MGP, O_MGA, O_END = 0, 512, 1024, 1408, 1664, 1696, 2208, 3232, 4256

VMEM_LIMIT = 56 * 1024 * 1024


def _const_spec(shape):
    nd = len(shape)
    return pl.BlockSpec(shape, lambda *_: (0,) * nd, pipeline_mode=pl.Buffered(1))


def _layer_norm(r, g, b):
    mu = jnp.mean(r, -1, keepdims=True)
    d = r - mu
    var = jnp.mean(d * d, -1, keepdims=True)
    return d * lax.rsqrt(var + NORM_EPS) * g + b


def _rms_norm(x, g):
    return x * lax.rsqrt(jnp.mean(x * x, -1, keepdims=True) + NORM_EPS) * g


def _silu(x):
    return x * jax.nn.sigmoid(x)


def _dot(a, b):
    return jnp.dot(a, b, preferred_element_type=F32)


def _dot_nt(a, b):
    return lax.dot_general(a, b, (((1,), (1,)), ((), ())), preferred_element_type=F32)


def _rope_block(blk, c, sa, sb):
    return blk * c + pltpu.roll(blk, HEAD_PAD - ROPE_HALF, 1) * sa + pltpu.roll(blk, ROPE_HALF, 1) * sb


def _pool_tail(pool_mean_minus_tok, pg, mgp, poolw_ref, pscale_ref, poolup_ref):
    ys = [_dot(m.astype(BF16), poolw_ref[g]) for g, m in enumerate(pool_mean_minus_tok)]
    y = jnp.concatenate(ys, axis=-1)
    yb = (y * pscale_ref[...] * _silu(pg)).astype(BF16)
    return jax.nn.sigmoid(mgp) * _dot(yb, poolup_ref[...])


def _q_heads(cq, qg_ref, wuq_ref, c, sa, sb):
    qf = _dot(_rms_norm(cq, qg_ref[...]).astype(BF16), wuq_ref[...])
    return [(_rope_block(qf[:, h * HEAD_PAD:(h + 1) * HEAD_PAD], c, sa, sb) * Q_SCALE).astype(BF16)
            for h in range(N_HEADS)]


def _rope_key(kr, c, sa, sb):
    return _rope_block(kr, c, sa, sb)


def _pre_prompt_kernel(x_ref, lnig_ref, lnib_ref, c_ref, sa_ref, sb_ref, w1t_ref, wkr_ref, poolw_ref, pscale_ref,
                       poolup_ref, qg_ref, wuq_ref, kvg_ref, wukt_ref, wuvt_ref,
                       q_ref, k_ref, vt_ref, lat_ref, krope_ref, gp_ref, sga_ref, ga_ref, pstate_ref,
                       ext_sc, *, tm, input_ln):
    si = pl.program_id(1)
    x = x_ref[0]
    if input_ln:
        x = _layer_norm(x, lnig_ref[...], lnib_ref[...])
    xb = x.astype(BF16)

    def seg(a, b):
        return _dot_nt(xb, w1t_ref[a:b, :])

    pv = seg(O_PV, O_PG)

    @pl.when(si == 0)
    def _():
        ext_sc[0:POOL_WMAX, :] = jnp.zeros((POOL_WMAX, POOL_WIDTH), F32)

    ext_sc[POOL_WMAX:POOL_WMAX + tm, :] = pv
    pos = si * tm + lax.broadcasted_iota(jnp.int32, (tm, 1), 0)
    means = []
    for g, w in enumerate(POOL_WINDOWS):
        c0 = g * POOL_GC
        acc = pv[:, c0:c0 + POOL_GC]
        for j in range(1, w):
            acc = acc + ext_sc[POOL_WMAX - j:POOL_WMAX - j + tm, c0:c0 + POOL_GC]
        cnt = jnp.minimum(pos + 1, w).astype(F32)
        means.append(acc / cnt - pv[:, c0:c0 + POOL_GC])
    tail = ext_sc[tm:tm + POOL_WMAX, :]
    pstate_ref[0] = tail
    ext_sc[0:POOL_WMAX, :] = tail
    gp_ref[0] = _pool_tail(means, seg(O_PG, O_CQ), seg(O_MGP, O_MGA), poolw_ref, pscale_ref,
                           poolup_ref).astype(BF16)
    sga_ref[0] = jax.nn.sigmoid(seg(O_MGA, O_END)).astype(BF16)
    ga_ref[0] = _silu(seg(O_AG, O_MGP)).astype(BF16)

    c, sa, sb = c_ref[...], sa_ref[...], sb_ref[...]
    for h, qh in enumerate(_q_heads(seg(O_CQ, O_CKV), qg_ref, wuq_ref, c, sa, sb)):
        q_ref[0, :, h * HEAD_PAD:(h + 1) * HEAD_PAD] = qh

    lat = _rms_norm(seg(O_CKV, O_KR), kvg_ref[...])
    lat_ref[0] = lat
    latb = lat.astype(BF16)
    krr = _rope_key(_dot_nt(xb, wkr_ref[...]), c, sa, sb)
    krope_ref[0] = krr[:, HEAD_PAD - QK_ROPE:]
    lane = lax.broadcasted_iota(jnp.int32, (tm, HEAD_PAD), 1)
    krk = jnp.where(lane < QK_HEAD, krr, 0.0)
    kn = _dot_nt(latb, wukt_ref[...])
    for h in range(N_HEADS):
        k_ref[0, :, h * HEAD_PAD:(h + 1) * HEAD_PAD] = (kn[:, h * HEAD_PAD:(h + 1) * HEAD_PAD] + krk).astype(BF16)
    vt = _dot_nt(wuvt_ref[...], latb)
    rowi = lax.broadcasted_iota(jnp.int32, (QK_PAD, tm), 0)
    vt_ref[0] = jnp.where((rowi & (HEAD_PAD - 1)) >= V_DIM, 1.0, vt).astype(BF16)


def _pre_prompt_call(x, ln_in, tabs, wl, tm, input_ln):
    bsz, seq, _ = x.shape
    row = lambda width: pl.BlockSpec((1, tm, width), lambda b, s: (b, s, 0))
    tab = pl.BlockSpec((tm, HEAD_PAD), lambda b, s: (s, 0))
    in_specs = [row(D_MODEL), _const_spec((1, D_MODEL)), _const_spec((1, D_MODEL)), tab, tab, tab,
                _const_spec((O_END, D_MODEL)), _const_spec((HEAD_PAD, D_MODEL)), _const_spec((4, POOL_GC, POOL_GC)),
                _const_spec((1, POOL_WIDTH)), _const_spec((POOL_WIDTH, D_MODEL)), _const_spec((1, Q_LORA)),
                _const_spec((Q_LORA, QK_PAD)), _const_spec((1, KV_LORA)), _const_spec((QK_PAD, KV_LORA)),
                _const_spec((QK_PAD, KV_LORA))]
    out_specs = [row(QK_PAD), row(QK_PAD), pl.BlockSpec((1, QK_PAD, tm), lambda b, s: (b, 0, s)),
                 row(KV_LORA), row(QK_ROPE), row(D_MODEL), row(D_MODEL),
                 row(ATTN_WIDTH), pl.BlockSpec((1, POOL_WMAX, POOL_WIDTH), lambda b, s: (b, 0, 0))]
    sds = jax.ShapeDtypeStruct
    out_shape = [sds((bsz, seq, QK_PAD), BF16), sds((bsz, seq, QK_PAD), BF16), sds((bsz, QK_PAD, seq), BF16),
                 sds((bsz, seq, KV_LORA), F32), sds((bsz, seq, QK_ROPE), F32), sds((bsz, seq, D_MODEL), BF16),
                 sds((bsz, seq, D_MODEL), BF16), sds((bsz, seq, ATTN_WIDTH), BF16),
                 sds((bsz, POOL_WMAX, POOL_WIDTH), F32)]
    return pl.pallas_call(
        functools.partial(_pre_prompt_kernel, tm=tm, input_ln=input_ln),
        grid=(bsz, seq // tm),
        in_specs=in_specs, out_specs=out_specs, out_shape=out_shape,
        scratch_shapes=[pltpu.VMEM((tm + POOL_WMAX, POOL_WIDTH), F32)],
        compiler_params=pltpu.CompilerParams(dimension_semantics=("arbitrary", "arbitrary"),
                                             vmem_limit_bytes=VMEM_LIMIT),
        name="prompt_pre",
    )(x, *ln_in, *tabs, wl["w1t"], wl["wkr"], wl["pool_w"], wl["pool_scale"], wl["pool_up"], wl["q_norm_g"], wl["wuq"],
      wl["kv_norm_g"], wl["wukt"], wl["wuvt"])


def _attn_kernel(q_ref, k_ref, vt_ref, o_ref, m_sc, acc_sc, ot_sc, *, bq, bk):
    qi = pl.program_id(1)
    ki = pl.program_id(2)
    kper = bq // bk

    @pl.when(ki == 0)
    def _():
        m_sc[...] = jnp.full(m_sc.shape, -jnp.inf, F32)
        acc_sc[...] = jnp.zeros(acc_sc.shape, F32)

    def step(masked):
        if masked:
            key_pos = ki * bk + lax.broadcasted_iota(jnp.int32, (bk, bq), 0)
            qry_pos = qi * bq + lax.broadcasted_iota(jnp.int32, (bk, bq), 1)
            keep = key_pos <= qry_pos
        def scores(h):
            sl = slice(h * HEAD_PAD, (h + 1) * HEAD_PAD)
            return _dot_nt(k_ref[0, :, sl], q_ref[0, :, sl]).astype(BF16)

        st_next = scores(0)
        for h in range(N_HEADS):
            sl = slice(h * HEAD_PAD, (h + 1) * HEAD_PAD)
            st = st_next
            if h + 1 < N_HEADS:
                st_next = scores(h + 1)
            if masked:
                st = jnp.where(keep, st, -jnp.inf)
            m_prev = m_sc[h]
            m_new = jnp.maximum(m_prev, jnp.max(st, axis=0, keepdims=True).astype(F32))
            alpha = jnp.exp2(m_prev - m_new)
            pt = jnp.exp2(st - m_new.astype(BF16))
            acc_sc[h] = alpha * acc_sc[h] + _dot(vt_ref[0, sl, :], pt)
            m_sc[h] = m_new

    @pl.when(ki < qi * kper)
    def _():
        step(False)

    @pl.when(jnp.logical_and(ki >= qi * kper, ki < (qi + 1) * kper))
    def _():
        step(True)

    @pl.when(ki == (qi + 1) * kper - 1)
    def _():
        for h in range(N_HEADS):
            acc = acc_sc[h]
            ot_sc[h * V_DIM:(h + 1) * V_DIM, :] = acc[:V_DIM, :] / acc[V_DIM:V_DIM + 1, :]
        o_ref[0] = ot_sc[...].T.astype(BF16)


def _attn_call(q, k, vt, bq, bk):
    bsz, seq, _ = q.shape
    kper = bq // bk
    qspec = pl.BlockSpec((1, bq, QK_PAD), lambda b, i, j: (b, i, 0))
    kspec = pl.BlockSpec((1, bk, QK_PAD), lambda b, i, j: (b, jnp.minimum(j, (i + 1) * kper - 1), 0))
    vspec = pl.BlockSpec((1, QK_PAD, bk), lambda b, i, j: (b, 0, jnp.minimum(j, (i + 1) * kper - 1)))
    return pl.pallas_call(
        functools.partial(_attn_kernel, bq=bq, bk=bk),
        grid=(bsz, seq // bq, seq // bk),
        in_specs=[qspec, kspec, vspec],
        out_specs=pl.BlockSpec((1, bq, ATTN_WIDTH), lambda b, i, j: (b, i, 0)),
        out_shape=jax.ShapeDtypeStruct((bsz, seq, ATTN_WIDTH), BF16),
        scratch_shapes=[pltpu.VMEM((N_HEADS, 1, bq), F32), pltpu.VMEM((N_HEADS, HEAD_PAD, bq), F32),
                        pltpu.VMEM((ATTN_WIDTH, bq), F32)],
        compiler_params=pltpu.CompilerParams(dimension_semantics=("arbitrary", "arbitrary", "arbitrary"),
                                             vmem_limit_bytes=VMEM_LIMIT),
        name="prompt_attention",
    )(q, k, vt)


def _post_kernel(o_ref, ga_ref, gp_ref, sga_ref, x_ref, p_ref, lnig_ref, lnib_ref, attnup_ref, wout_ref, pgate_ref,
                 pproj_ref, lng_ref, lnb_ref, y_ref, *, input_ln):
    og = (o_ref[...].astype(F32) * ga_ref[...].astype(F32)).astype(BF16)
    b_attn = _dot(og, attnup_ref[...])
    mix = gp_ref[...].astype(F32) + sga_ref[...].astype(F32) * b_attn
    sub = _dot(mix.astype(BF16), wout_ref[...])
    x = x_ref[...]
    if input_ln:
        x = _layer_norm(x, lnig_ref[...], lnib_ref[...])
    ple = jax.nn.sigmoid(_dot(x.astype(BF16), pgate_ref[...])) * _dot(p_ref[...].astype(BF16), pproj_ref[...])
    y_ref[...] = _layer_norm(DN_ALPHA * x + sub + ple, lng_ref[...], lnb_ref[...])


def _post_call(o, ga, gp, sga, x, p, ln_in, wl, tm, input_ln):
    n = x.shape[0]
    row = lambda width: pl.BlockSpec((tm, width), lambda i: (i, 0))
    return pl.pallas_call(
        functools.partial(_post_kernel, input_ln=input_ln),
        grid=(n // tm,),
        in_specs=[row(ATTN_WIDTH), row(ATTN_WIDTH), row(D_MODEL), row(D_MODEL), row(D_MODEL), row(PLE_DIM),
                  _const_spec((1, D_MODEL)), _const_spec((1, D_MODEL)),
                  _const_spec((ATTN_WIDTH, D_MODEL)), _const_spec((D_MODEL, D_MODEL)),
                  _const_spec((D_MODEL, D_MODEL)), _const_spec((PLE_DIM, D_MODEL)),
                  _const_spec((1, D_MODEL)), _const_spec((1, D_MODEL))],
        out_specs=row(D_MODEL),
        out_shape=jax.ShapeDtypeStruct((n, D_MODEL), F32),
        compiler_params=pltpu.CompilerParams(dimension_semantics=("arbitrary",), vmem_limit_bytes=VMEM_LIMIT),
        name="layer_post",
    )(o, ga, gp, sga, x, p, *ln_in, wl["attn_up"], wl["w_out"], wl["ple_gate"], wl["ple_proj"], wl["ln_g"],
      wl["ln_b"])


def _pre_sample_kernel(x_ref, lnig_ref, lnib_ref, state_ref, c_ref, sa_ref, sb_ref, w1t_ref, wkr_ref, poolw_ref,
                       pscale_ref, poolup_ref, qg_ref, wuq_ref, kvg_ref, wukt_ref,
                       q_ref, qlat_ref, lat_ref, krope_ref, gp_ref, sga_ref, ga_ref, pv_ref, *, input_ln):
    x = x_ref[...]
    if input_ln:
        x = _layer_norm(x, lnig_ref[...], lnib_ref[...])
    xb = x.astype(BF16)

    def seg(a, b):
        return _dot_nt(xb, w1t_ref[a:b, :])

    pv = seg(O_PV, O_PG)
    pv_ref[...] = pv
    means = []
    for g, w in enumerate(POOL_WINDOWS):
        c0 = g * POOL_GC
        acc = pv[:, c0:c0 + POOL_GC]
        for j in range(1, w):
            acc = acc + state_ref[POOL_BUF - j, :, c0:c0 + POOL_GC]
        means.append(acc / float(w) - pv[:, c0:c0 + POOL_GC])
    gp_ref[...] = _pool_tail(means, seg(O_PG, O_CQ), seg(O_MGP, O_MGA), poolw_ref, pscale_ref,
                             poolup_ref).astype(BF16)
    sga_ref[...] = jax.nn.sigmoid(seg(O_MGA, O_END)).astype(BF16)
    ga_ref[...] = _silu(seg(O_AG, O_MGP)).astype(BF16)

    c, sa, sb = c_ref[...], sa_ref[...], sb_ref[...]
    qh = _q_heads(seg(O_CQ, O_CKV), qg_ref, wuq_ref, c, sa, sb)
    qf = jnp.concatenate(qh, axis=-1)
    q_ref[...] = qf
    qlat_ref[...] = _dot(qf, wukt_ref[...]).astype(BF16)
    lat_ref[...] = _rms_norm(seg(O_CKV, O_KR), kvg_ref[...])
    krope_ref[...] = _rope_key(_dot_nt(xb, wkr_ref[...]), c, sa, sb)[:, HEAD_PAD - QK_ROPE:]


def _pre_sample_call(x, ln_in, state_t, tabs, wl, input_ln):
    n = x.shape[0]
    sds = jax.ShapeDtypeStruct
    out_shape = [sds((n, QK_PAD), BF16), sds((n, N_HEADS * KV_LORA), BF16), sds((n, KV_LORA), F32),
                 sds((n, QK_ROPE), F32), sds((n, D_MODEL), BF16), sds((n, D_MODEL), BF16),
                 sds((n, ATTN_WIDTH), BF16), sds((n, POOL_WIDTH), F32)]
    return pl.pallas_call(
        functools.partial(_pre_sample_kernel, input_ln=input_ln),
        out_shape=out_shape,
        compiler_params=pltpu.CompilerParams(vmem_limit_bytes=VMEM_LIMIT),
        name="sample_pre",
    )(x, *ln_in, state_t, *tabs, wl["w1t"], wl["wkr"], wl["pool_w"], wl["pool_scale"], wl["pool_up"], wl["q_norm_g"], wl["wuq"],
      wl["kv_norm_g"], wl["wukt_bd"])


DEC_CHUNK = 1024


def _decode_kernel(pt_ref, qlat_ref, qrope_ref, latn_ref, ropen_ref, clat_hbm, cropet_hbm, o_ref,
                   latbuf, ropebuf, latb16, sem, *, layer, n_pages):
    b = pl.program_id(0)
    nb = pl.num_programs(0)
    slot = b % 2

    def page_copies(sl, p, page):
        rows = pl.ds(p * PAGE_SIZE, PAGE_SIZE)
        return (pltpu.make_async_copy(clat_hbm.at[layer, page], latbuf.at[sl, rows], sem.at[0, sl]),
                pltpu.make_async_copy(cropet_hbm.at[layer, page], ropebuf.at[sl, p], sem.at[1, sl]))

    def start_all(bb, sl):
        def body(p, carry):
            for cp in page_copies(sl, p, pt_ref[bb, p]):
                cp.start()
            return carry
        lax.fori_loop(0, n_pages, body, 0)

    @pl.when(b == 0)
    def _():
        start_all(0, 0)

    @pl.when(b + 1 < nb)
    def _():
        start_all(b + 1, 1 - slot)

    ql = qlat_ref[0]
    qr = qrope_ref[0]
    latn = latn_ref[0]
    ropen = ropen_ref[0]
    for p in range(n_pages):
        for cp in page_copies(slot, p, 0):
            cp.wait()
    latb16[...] = latbuf[slot].astype(BF16)
    s_rope = jnp.concatenate([_dot(qr, ropebuf[slot, p].astype(BF16)) for p in range(n_pages)], axis=1)
    s = _dot_nt(ql, latb16[...]) + s_rope
    s_new = (jnp.sum(ql.astype(F32) * latn, -1, keepdims=True)
             + jnp.sum(qr.astype(F32) * ropen, -1, keepdims=True))
    m = jnp.maximum(jnp.max(s, -1, keepdims=True), s_new)
    p = jnp.exp2(s - m)
    p_new = jnp.exp2(s_new - m)
    denom = jnp.sum(p, -1, keepdims=True) + p_new
    o_ref[0] = (_dot(p.astype(BF16), latb16[...]) + p_new * latn) / denom


def _decode_call(page_table, qlat, qrope, lat_new, rope_new, cache_lat, cache_rope, layer):
    bsz, n_pages = page_table.shape
    past = n_pages * PAGE_SIZE
    blk = lambda s1, s2: pl.BlockSpec((1, s1, s2), lambda b, pt: (b, 0, 0))
    grid_spec = pltpu.PrefetchScalarGridSpec(
        num_scalar_prefetch=1,
        grid=(bsz,),
        in_specs=[blk(N_HEADS, KV_LORA), blk(N_HEADS, QK_ROPE), blk(1, KV_LORA), blk(1, QK_ROPE),
                  pl.BlockSpec(memory_space=pl.ANY), pl.BlockSpec(memory_space=pl.ANY)],
        out_specs=blk(N_HEADS, KV_LORA),
        scratch_shapes=[pltpu.VMEM((2, past, KV_LORA), F32), pltpu.VMEM((2, n_pages, QK_ROPE, PAGE_SIZE), F32),
                        pltpu.VMEM((past, KV_LORA), BF16), pltpu.SemaphoreType.DMA((2, 2))],
    )
    return pl.pallas_call(
        functools.partial(_decode_kernel, layer=layer, n_pages=n_pages),
        grid_spec=grid_spec,
        out_shape=jax.ShapeDtypeStruct((bsz, N_HEADS, KV_LORA), F32),
        compiler_params=pltpu.CompilerParams(dimension_semantics=("arbitrary",), vmem_limit_bytes=VMEM_LIMIT),
        name="sample_decode_attention",
    )(page_table, qlat, qrope, lat_new, rope_new, cache_lat, cache_rope)


def _olat_kernel(olat_ref, wuv_ref, o_ref):
    o_ref[...] = _dot(olat_ref[...].astype(BF16), wuv_ref[...]).astype(BF16)


def _olat_call(olat_flat, wuv_bd):
    n = olat_flat.shape[0]
    return pl.pallas_call(
        _olat_kernel,
        out_shape=jax.ShapeDtypeStruct((n, ATTN_WIDTH), BF16),
        name="sample_value_up",
    )(olat_flat, wuv_bd)


def _rope_tables(pos):
    freqs = ROPE_THETA ** (-jnp.arange(ROPE_HALF, dtype=F32) / ROPE_HALF)
    ang = pos.astype(F32)[:, None] * freqs[None, :]
    cos, sin = jnp.cos(ang), jnp.sin(ang)
    one = jnp.ones((pos.shape[0], QK_NOPE), F32)
    z64 = jnp.zeros((pos.shape[0], QK_NOPE), F32)
    z16 = jnp.zeros((pos.shape[0], ROPE_HALF), F32)
    c = jnp.concatenate([one, cos, cos, cos, cos], -1)
    sa = jnp.concatenate([z64, -sin, z16, -sin, z16], -1)
    sb = jnp.concatenate([z64, z16, sin, z16, sin], -1)
    return c, sa, sb


def _prep_layer(i, w_in, pool_w, pool_scale, pool_up, q_norm_g, w_uq, kv_norm_g, w_uk, w_uv, attn_up, w_out,
                ple_proj, ple_gate, ln_g, ln_b):
    w1t = jnp.swapaxes(w_in[i], 0, 1)
    wkr_t = w1t[O_KR:O_AG]
    wkr = jnp.concatenate([jnp.zeros((QK_NOPE, D_MODEL), F32), wkr_t, wkr_t], axis=0)
    pad_head = lambda a: jnp.pad(a, ((0, 0), (0, 0), (0, HEAD_PAD - a.shape[-1]))).reshape(a.shape[0], QK_PAD)
    eye = jnp.eye(N_HEADS, dtype=F32)
    wukt = jnp.pad(jnp.transpose(w_uk[i], (1, 2, 0)), ((0, 0), (0, HEAD_PAD - QK_NOPE), (0, 0)))
    wuvt = jnp.pad(jnp.transpose(w_uv[i], (1, 2, 0)), ((0, 0), (0, HEAD_PAD - V_DIM), (0, 0)))
    wukt_bd = (wukt[:, :, None, :] * eye[:, None, :, None]).reshape(QK_PAD, N_HEADS * KV_LORA)
    wuv_bd = (jnp.transpose(w_uv[i], (1, 0, 2))[:, :, None, :] * eye[:, None, :, None]).reshape(
        N_HEADS * KV_LORA, ATTN_WIDTH)
    return {
        "w1t": w1t.astype(BF16),
        "wkr": wkr.astype(BF16),
        "pool_w": pool_w[i].astype(BF16),
        "pool_scale": pool_scale[i].reshape(1, POOL_WIDTH),
        "pool_up": pool_up[i].astype(BF16),
        "q_norm_g": q_norm_g[i].reshape(1, Q_LORA),
        "wuq": pad_head(w_uq[i].reshape(Q_LORA, N_HEADS, QK_HEAD)).astype(BF16),
        "kv_norm_g": kv_norm_g[i].reshape(1, KV_LORA),
        "wukt": wukt.reshape(QK_PAD, KV_LORA).astype(BF16),
        "wuvt": wuvt.reshape(QK_PAD, KV_LORA).astype(BF16),
        "wukt_bd": wukt_bd.astype(BF16),
        "wuv_bd": wuv_bd.astype(BF16),
        "attn_up": attn_up[i].astype(BF16),
        "w_out": w_out[i].astype(BF16),
        "ple_gate": ple_gate[i].astype(BF16),
        "ple_proj": ple_proj[i].astype(BF16),
        "ln_g": ln_g[i].reshape(1, D_MODEL),
        "ln_b": ln_b[i].reshape(1, D_MODEL),
    }


def kernel(x_prompt, x_sample, cache_kv_latent, cache_k_rope, state_pool, page_table, p_prompt, p_sample,
           ln_in_g, ln_in_b, w_in, pool_w, pool_scale, pool_up, q_norm_g, w_uq, kv_norm_g, w_uk, w_uv,
           attn_up, w_out, ple_proj, ple_gate, ln_g, ln_b):
    bsz, seq, _ = x_prompt.shape
    nsmp = x_sample.shape[0]
    n_pages = page_table.shape[1]
    past_len = n_pages * PAGE_SIZE
    n_tok = bsz * seq
    tm_pre, tm_post, bq, bk = 512, 512, 1024, 1024

    tabs_p = _rope_tables(jnp.arange(seq))
    tabs_s = _rope_tables(jnp.full((1,), past_len))
    ln_in = (ln_in_g.reshape(1, D_MODEL), ln_in_b.reshape(1, D_MODEL))
    cache_rope_t = jnp.swapaxes(cache_k_rope, 2, 3)

    hp = x_prompt.reshape(n_tok, D_MODEL)
    hs = x_sample.reshape(nsmp, D_MODEL)

    lat_p, rope_p, pool_p, lat_s, rope_s, pool_s = [], [], [], [], [], []
    for i in range(DEPTH):
        first = i == 0
        wl = _prep_layer(i, w_in, pool_w, pool_scale, pool_up, q_norm_g, w_uq, kv_norm_g, w_uk, w_uv, attn_up,
                         w_out, ple_proj, ple_gate, ln_g, ln_b)
        q, k, vt, lat, krope, gp, sga, ga, pstate = _pre_prompt_call(hp.reshape(bsz, seq, D_MODEL), ln_in, tabs_p,
                                                                     wl, tm_pre, first)
        o = _attn_call(q, k, vt, bq, bk)
        hp = _post_call(o.reshape(n_tok, ATTN_WIDTH), ga.reshape(n_tok, ATTN_WIDTH), gp.reshape(n_tok, D_MODEL),
                        sga.reshape(n_tok, D_MODEL), hp, p_prompt[i].reshape(n_tok, PLE_DIM), ln_in, wl, tm_post,
                        first)
        lat_p.append(lat)
        rope_p.append(krope)
        pool_p.append(pstate[:, 1:, :])
        qs, qlat, lat_n, rope_n, gp_s, sga_s, ga_s, pv_s = _pre_sample_call(
            hs, ln_in, jnp.swapaxes(state_pool[i], 0, 1), tabs_s, wl, first)
        qrope = qs.reshape(nsmp, N_HEADS, HEAD_PAD)[:, :, QK_NOPE:QK_HEAD]
        olat = _decode_call(page_table, qlat.reshape(nsmp, N_HEADS, KV_LORA), qrope,
                            lat_n.reshape(nsmp, 1, KV_LORA), rope_n.reshape(nsmp, 1, QK_ROPE),
                            cache_kv_latent, cache_rope_t, i)
        o_s = _olat_call(olat.reshape(nsmp, N_HEADS * KV_LORA), wl["wuv_bd"])
        hs = _post_call(o_s, ga_s, gp_s, sga_s, hs, p_sample[i].reshape(nsmp, PLE_DIM), ln_in, wl, nsmp, first)
        lat_s.append(lat_n.reshape(nsmp, 1, KV_LORA))
        rope_s.append(rope_n.reshape(nsmp, 1, QK_ROPE))
        pool_s.append(jnp.concatenate([state_pool[i][:, 1:, :], pv_s[:, None, :]], axis=1))

    return (hp.reshape(bsz, seq, D_MODEL), hs.reshape(nsmp, 1, D_MODEL), jnp.stack(lat_p), jnp.stack(rope_p),
            jnp.stack(pool_p), jnp.stack(lat_s), jnp.stack(rope_s), jnp.stack(pool_s))
```

```python
import functools
import math

import jax
import jax.numpy as jnp
from jax import lax
from jax.experimental import pallas as pl
from jax.experimental.pallas import tpu as pltpu

F32 = jnp.float32
BF16 = jnp.bfloat16

D_MODEL = 1024
DEPTH = 2
PAGE_SIZE = 128
POOL_WINDOWS = (2, 4, 8, 16)
POOL_WIDTH = 512
POOL_GC = 128
POOL_WMAX = 16
POOL_BUF = POOL_WMAX - 1
N_HEADS = 8
QK_NOPE = 64
QK_ROPE = 32
ROPE_HALF = QK_ROPE // 2
QK_HEAD = QK_NOPE + QK_ROPE
V_DIM = 64
Q_LORA = 384
KV_LORA = 256
ATTN_WIDTH = N_HEADS * V_DIM
ROPE_THETA = 10000.0
PLE_DIM = 256
DN_ALPHA = (2 * DEPTH) ** 0.25
NORM_EPS = 1e-6
HEAD_PAD = 128
QK_PAD = N_HEADS * HEAD_PAD
Q_SCALE = (QK_HEAD ** -0.5) * math.log2(math.e)

O_PV, O_PG, O_CQ, O_CKV, O_KR, O_AG, O_MGP, O_MGA, O_END = 0, 512, 1024, 1408, 1664, 1696, 2208, 3232, 4256

VMEM_LIMIT = 56 * 1024 * 1024


def _const_spec(shape):
    nd = len(shape)
    return pl.BlockSpec(shape, lambda *_: (0,) * nd, pipeline_mode=pl.Buffered(1))


def _layer_norm(r, g, b):
    mu = jnp.mean(r, -1, keepdims=True)
    d = r - mu
    var = jnp.mean(d * d, -1, keepdims=True)
    return d * lax.rsqrt(var + NORM_EPS) * g + b


def _rms_norm(x, g):
    return x * lax.rsqrt(jnp.mean(x * x, -1, keepdims=True) + NORM_EPS) * g


def _silu(x):
    return x * jax.nn.sigmoid(x)


def _dot(a, b):
    return jnp.dot(a, b, preferred_element_type=F32)


def _dot_nt(a, b):
    return lax.dot_general(a, b, (((1,), (1,)), ((), ())), preferred_element_type=F32)


def _rope_block(blk, c, sa, sb):
    return blk * c + pltpu.roll(blk, HEAD_PAD - ROPE_HALF, 1) * sa + pltpu.roll(blk, ROPE_HALF, 1) * sb


def _pool_tail(pool_mean_minus_tok, pg, mgp, poolw_ref, pscale_ref, poolup_ref):
    ys = [_dot(m.astype(BF16), poolw_ref[g]) for g, m in enumerate(pool_mean_minus_tok)]
    y = jnp.concatenate(ys, axis=-1)
    yb = (y * pscale_ref[...] * _silu(pg)).astype(BF16)
    return jax.nn.sigmoid(mgp) * _dot(yb, poolup_ref[...])


def _q_heads(cq, qg_ref, wuq_ref, c, sa, sb):
    qf = _dot(_rms_norm(cq, qg_ref[...]).astype(BF16), wuq_ref[...])
    return [(_rope_block(qf[:, h * HEAD_PAD:(h + 1) * HEAD_PAD], c, sa, sb) * Q_SCALE).astype(BF16)
            for h in range(N_HEADS)]


def _rope_key(kr, c, sa, sb):
    return _rope_block(kr, c, sa, sb)


def _pre_prompt_kernel(x_ref, lnig_ref, lnib_ref, c_ref, sa_ref, sb_ref, w1t_ref, wkr_ref, poolw_ref, pscale_ref,
                       poolup_ref, qg_ref, wuq_ref, kvg_ref, wukt_ref, wuvt_ref,
                       q_ref, k_ref, vt_ref, lat_ref, krope_ref, gp_ref, sga_ref, ga_ref, pstate_ref,
                       ext_sc, *, tm, input_ln):
    si = pl.program_id(1)
    x = x_ref[0]
    if input_ln:
        x = _layer_norm(x, lnig_ref[...], lnib_ref[...])
    xb = x.astype(BF16)

    def seg(a, b):
        return _dot_nt(xb, w1t_ref[a:b, :])

    pv = seg(O_PV, O_PG)

    @pl.when(si == 0)
    def _():
        ext_sc[0:POOL_WMAX, :] = jnp.zeros((POOL_WMAX, POOL_WIDTH), F32)

    ext_sc[POOL_WMAX:POOL_WMAX + tm, :] = pv
    pos = si * tm + lax.broadcasted_iota(jnp.int32, (tm, 1), 0)
    means = []
    for g, w in enumerate(POOL_WINDOWS):
        c0 = g * POOL_GC
        acc = pv[:, c0:c0 + POOL_GC]
        for j in range(1, w):
            acc = acc + ext_sc[POOL_WMAX - j:POOL_WMAX - j + tm, c0:c0 + POOL_GC]
        cnt = jnp.minimum(pos + 1, w).astype(F32)
        means.append(acc / cnt - pv[:, c0:c0 + POOL_GC])
    tail = ext_sc[tm:tm + POOL_WMAX, :]
    pstate_ref[0] = tail
    ext_sc[0:POOL_WMAX, :] = tail
    gp_ref[0] = _pool_tail(means, seg(O_PG, O_CQ), seg(O_MGP, O_MGA), poolw_ref, pscale_ref,
                           poolup_ref).astype(BF16)
    sga_ref[0] = jax.nn.sigmoid(seg(O_MGA, O_END)).astype(BF16)
    ga_ref[0] = _silu(seg(O_AG, O_MGP)).astype(BF16)

    c, sa, sb = c_ref[...], sa_ref[...], sb_ref[...]
    for h, qh in enumerate(_q_heads(seg(O_CQ, O_CKV), qg_ref, wuq_ref, c, sa, sb)):
        q_ref[0, :, h * HEAD_PAD:(h + 1) * HEAD_PAD] = qh

    lat = _rms_norm(seg(O_CKV, O_KR), kvg_ref[...])
    lat_ref[0] = lat
    latb = lat.astype(BF16)
    krr = _rope_key(_dot_nt(xb, wkr_ref[...]), c, sa, sb)
    krope_ref[0] = krr[:, HEAD_PAD - QK_ROPE:]
    lane = lax.broadcasted_iota(jnp.int32, (tm, HEAD_PAD), 1)
    krk = jnp.where(lane < QK_HEAD, krr, 0.0)
    kn = _dot_nt(latb, wukt_ref[...])
    for h in range(N_HEADS):
        k_ref[0, :, h * HEAD_PAD:(h + 1) * HEAD_PAD] = (kn[:, h * HEAD_PAD:(h + 1) * HEAD_PAD] + krk).astype(BF16)
    vt = _dot_nt(wuvt_ref[...], latb)
    rowi = lax.broadcasted_iota(jnp.int32, (QK_PAD, tm), 0)
    vt_ref[0] = jnp.where((rowi & (HEAD_PAD - 1)) >= V_DIM, 1.0, vt).astype(BF16)


def _pre_prompt_call(x, ln_in, tabs, wl, tm, input_ln):
    bsz, seq, _ = x.shape
    row = lambda width: pl.BlockSpec((1, tm, width), lambda b, s: (b, s, 0))
    tab = pl.BlockSpec((tm, HEAD_PAD), lambda b, s: (s, 0))
    in_specs = [row(D_MODEL), _const_spec((1, D_MODEL)), _const_spec((1, D_MODEL)), tab, tab, tab,
                _const_spec((O_END, D_MODEL)), _const_spec((HEAD_PAD, D_MODEL)), _const_spec((4, POOL_GC, POOL_GC)),
                _const_spec((1, POOL_WIDTH)), _const_spec((POOL_WIDTH, D_MODEL)), _const_spec((1, Q_LORA)),
                _const_spec((Q_LORA, QK_PAD)), _const_spec((1, KV_LORA)), _const_spec((QK_PAD, KV_LORA)),
                _const_spec((QK_PAD, KV_LORA))]
    out_specs = [row(QK_PAD), row(QK_PAD), pl.BlockSpec((1, QK_PAD, tm), lambda b, s: (b, 0, s)),
                 row(KV_LORA), row(QK_ROPE), row(D_MODEL), row(D_MODEL),
                 row(ATTN_WIDTH), pl.BlockSpec((1, POOL_WMAX, POOL_WIDTH), lambda b, s: (b, 0, 0))]
    sds = jax.ShapeDtypeStruct
    out_shape = [sds((bsz, seq, QK_PAD), BF16), sds((bsz, seq, QK_PAD), BF16), sds((bsz, QK_PAD, seq), BF16),
                 sds((bsz, seq, KV_LORA), F32), sds((bsz, seq, QK_ROPE), F32), sds((bsz, seq, D_MODEL), BF16),
                 sds((bsz, seq, D_MODEL), BF16), sds((bsz, seq, ATTN_WIDTH), BF16),
                 sds((bsz, POOL_WMAX, POOL_WIDTH), F32)]
    return pl.pallas_call(
        functools.partial(_pre_prompt_kernel, tm=tm, input_ln=input_ln),
        grid=(bsz, seq // tm),
        in_specs=in_specs, out_specs=out_specs, out_shape=out_shape,
        scratch_shapes=[pltpu.VMEM((tm + POOL_WMAX, POOL_WIDTH), F32)],
        compiler_params=pltpu.CompilerParams(dimension_semantics=("arbitrary", "arbitrary"),
                                             vmem_limit_bytes=VMEM_LIMIT),
        name="prompt_pre",
    )(x, *ln_in, *tabs, wl["w1t"], wl["wkr"], wl["pool_w"], wl["pool_scale"], wl["pool_up"], wl["q_norm_g"], wl["wuq"],
      wl["kv_norm_g"], wl["wukt"], wl["wuvt"])


def _attn_kernel(q_ref, k_ref, vt_ref, o_ref, m_sc, acc_sc, ot_sc, *, bq, bk):
    qi = pl.program_id(1)
    ki = pl.program_id(2)
    kper = bq // bk

    @pl.when(ki == 0)
    def _():
        m_sc[...] = jnp.full(m_sc.shape, -jnp.inf, F32)
        acc_sc[...] = jnp.zeros(acc_sc.shape, F32)

    def piece(qlo, qhi, khi, masked):
        qs = slice(qlo, qhi)
        if masked:
            key_pos = ki * bk + lax.broadcasted_iota(jnp.int32, (khi, qhi - qlo), 0)
            qry_pos = qi * bq + qlo + lax.broadcasted_iota(jnp.int32, (khi, qhi - qlo), 1)
            keep = key_pos <= qry_pos

        def scores(h):
            sl = slice(h * HEAD_PAD, (h + 1) * HEAD_PAD)
            return _dot_nt(k_ref[0, :khi, sl], q_ref[0, qs, sl]).astype(BF16)

        st_next = scores(0)
        for h in range(N_HEADS):
            sl = slice(h * HEAD_PAD, (h + 1) * HEAD_PAD)
            st = st_next
            if h + 1 < N_HEADS:
                st_next = scores(h + 1)
            if masked:
                st = jnp.where(keep, st, -jnp.inf)
            m_prev = m_sc[h, :, qs]
            m_new = jnp.maximum(m_prev, jnp.max(st, axis=0, keepdims=True).astype(F32))
            alpha = jnp.exp2(m_prev - m_new)
            pt = jnp.exp2(st - m_new.astype(BF16))
            acc_sc[h, :, qs] = alpha * acc_sc[h, :, qs] + _dot(vt_ref[0, sl, :khi], pt)
            m_sc[h, :, qs] = m_new

    @pl.when(ki < qi * kper)
    def _():
        piece(0, bq, bk, False)

    @pl.when(jnp.logical_and(ki >= qi * kper, ki < (qi + 1) * kper))
    def _():
        if bq == bk:
            piece(0, bq // 2, bk // 2, True)
            piece(bq // 2, bq, bk, True)
        else:
            piece(0, bq, bk, True)

    @pl.when(ki == (qi + 1) * kper - 1)
    def _():
        for h in range(N_HEADS):
            acc = acc_sc[h]
            ot_sc[h * V_DIM:(h + 1) * V_DIM, :] = acc[:V_DIM, :] / acc[V_DIM:V_DIM + 1, :]
        o_ref[0] = ot_sc[...].T.astype(BF16)


def _attn_call(q, k, vt, bq, bk):
    bsz, seq, _ = q.shape
    kper = bq // bk
    qspec = pl.BlockSpec((1, bq, QK_PAD), lambda b, i, j: (b, i, 0))
    kspec = pl.BlockSpec((1, bk, QK_PAD), lambda b, i, j: (b, jnp.minimum(j, (i + 1) * kper - 1), 0))
    vspec = pl.BlockSpec((1, QK_PAD, bk), lambda b, i, j: (b, 0, jnp.minimum(j, (i + 1) * kper - 1)))
    return pl.pallas_call(
        functools.partial(_attn_kernel, bq=bq, bk=bk),
        grid=(bsz, seq // bq, seq // bk),
        in_specs=[qspec, kspec, vspec],
        out_specs=pl.BlockSpec((1, bq, ATTN_WIDTH), lambda b, i, j: (b, i, 0)),
        out_shape=jax.ShapeDtypeStruct((bsz, seq, ATTN_WIDTH), BF16),
        scratch_shapes=[pltpu.VMEM((N_HEADS, 1, bq), F32), pltpu.VMEM((N_HEADS, HEAD_PAD, bq), F32),
                        pltpu.VMEM((ATTN_WIDTH, bq), F32)],
        compiler_params=pltpu.CompilerParams(dimension_semantics=("arbitrary", "arbitrary", "arbitrary"),
                                             vmem_limit_bytes=VMEM_LIMIT),
        name="prompt_attention",
    )(q, k, vt)


def _post_kernel(o_ref, ga_ref, gp_ref, sga_ref, x_ref, p_ref, lnig_ref, lnib_ref, attnup_ref, wout_ref, pgate_ref,
                 pproj_ref, lng_ref, lnb_ref, y_ref, *, input_ln):
    og = (o_ref[...].astype(F32) * ga_ref[...].astype(F32)).astype(BF16)
    b_attn = _dot(og, attnup_ref[...])
    mix = gp_ref[...].astype(F32) + sga_ref[...].astype(F32) * b_attn
    sub = _dot(mix.astype(BF16), wout_ref[...])
    x = x_ref[...]
    if input_ln:
        x = _layer_norm(x, lnig_ref[...], lnib_ref[...])
    ple = jax.nn.sigmoid(_dot(x.astype(BF16), pgate_ref[...])) * _dot(p_ref[...].astype(BF16), pproj_ref[...])
    y_ref[...] = _layer_norm(DN_ALPHA * x + sub + ple, lng_ref[...], lnb_ref[...])


def _post_call(o, ga, gp, sga, x, p, ln_in, wl, tm, input_ln):
    n = x.shape[0]
    row = lambda width: pl.BlockSpec((tm, width), lambda i: (i, 0))
    return pl.pallas_call(
        functools.partial(_post_kernel, input_ln=input_ln),
        grid=(n // tm,),
        in_specs=[row(ATTN_WIDTH), row(ATTN_WIDTH), row(D_MODEL), row(D_MODEL), row(D_MODEL), row(PLE_DIM),
                  _const_spec((1, D_MODEL)), _const_spec((1, D_MODEL)),
                  _const_spec((ATTN_WIDTH, D_MODEL)), _const_spec((D_MODEL, D_MODEL)),
                  _const_spec((D_MODEL, D_MODEL)), _const_spec((PLE_DIM, D_MODEL)),
                  _const_spec((1, D_MODEL)), _const_spec((1, D_MODEL))],
        out_specs=row(D_MODEL),
        out_shape=jax.ShapeDtypeStruct((n, D_MODEL), F32),
        compiler_params=pltpu.CompilerParams(dimension_semantics=("arbitrary",), vmem_limit_bytes=VMEM_LIMIT),
        name="layer_post",
    )(o, ga, gp, sga, x, p, *ln_in, wl["attn_up"], wl["w_out"], wl["ple_gate"], wl["ple_proj"], wl["ln_g"],
      wl["ln_b"])


def _pre_sample_kernel(x_ref, lnig_ref, lnib_ref, state_ref, c_ref, sa_ref, sb_ref, w1t_ref, wkr_ref, poolw_ref,
                       pscale_ref, poolup_ref, qg_ref, wuq_ref, kvg_ref, wukt_ref,
                       q_ref, qlat_ref, lat_ref, krope_ref, gp_ref, sga_ref, ga_ref, pv_ref, *, input_ln):
    x = x_ref[...]
    if input_ln:
        x = _layer_norm(x, lnig_ref[...], lnib_ref[...])
    xb = x.astype(BF16)

    def seg(a, b):
        return _dot_nt(xb, w1t_ref[a:b, :])

    pv = seg(O_PV, O_PG)
    pv_ref[...] = pv
    means = []
    for g, w in enumerate(POOL_WINDOWS):
        c0 = g * POOL_GC
        acc = pv[:, c0:c0 + POOL_GC]
        for j in range(1, w):
            acc = acc + state_ref[POOL_BUF - j, :, c0:c0 + POOL_GC]
        means.append(acc / float(w) - pv[:, c0:c0 + POOL_GC])
    gp_ref[...] = _pool_tail(means, seg(O_PG, O_CQ), seg(O_MGP, O_MGA), poolw_ref, pscale_ref,
                             poolup_ref).astype(BF16)
    sga_ref[...] = jax.nn.sigmoid(seg(O_MGA, O_END)).astype(BF16)
    ga_ref[...] = _silu(seg(O_AG, O_MGP)).astype(BF16)

    c, sa, sb = c_ref[...], sa_ref[...], sb_ref[...]
    qh = _q_heads(seg(O_CQ, O_CKV), qg_ref, wuq_ref, c, sa, sb)
    qf = jnp.concatenate(qh, axis=-1)
    q_ref[...] = qf
    for h in range(N_HEADS):
        qlat_ref[:, h * KV_LORA:(h + 1) * KV_LORA] = _dot(
            qh[h], wukt_ref[h * HEAD_PAD:(h + 1) * HEAD_PAD, :]).astype(BF16)
    lat_ref[...] = _rms_norm(seg(O_CKV, O_KR), kvg_ref[...])
    krope_ref[...] = _rope_key(_dot_nt(xb, wkr_ref[...]), c, sa, sb)[:, HEAD_PAD - QK_ROPE:]


def _pre_sample_call(x, ln_in, state_t, tabs, wl, input_ln):
    n = x.shape[0]
    sds = jax.ShapeDtypeStruct
    out_shape = [sds((n, QK_PAD), BF16), sds((n, N_HEADS * KV_LORA), BF16), sds((n, KV_LORA), F32),
                 sds((n, QK_ROPE), F32), sds((n, D_MODEL), BF16), sds((n, D_MODEL), BF16),
                 sds((n, ATTN_WIDTH), BF16), sds((n, POOL_WIDTH), F32)]
    return pl.pallas_call(
        functools.partial(_pre_sample_kernel, input_ln=input_ln),
        out_shape=out_shape,
        compiler_params=pltpu.CompilerParams(vmem_limit_bytes=VMEM_LIMIT),
        name="sample_pre",
    )(x, *ln_in, state_t, *tabs, wl["w1t"], wl["wkr"], wl["pool_w"], wl["pool_scale"], wl["pool_up"], wl["q_norm_g"], wl["wuq"],
      wl["kv_norm_g"], wl["wukt"])


def _decode_kernel(pt_ref, qlat_ref, qrope_ref, latn_ref, ropen_ref, clat_hbm, cropet_hbm, o_ref,
                   latbuf, ropebuf, latb16, p_sc, tail_sc, den_sc, sem, *, layer, n_pages):
    g = pl.program_id(0)
    nb = pl.num_programs(0) - 1
    slot = g % 2
    prev = 1 - slot
    past = n_pages * PAGE_SIZE

    def page_copies(sl, p, page):
        rows = pl.ds(p * PAGE_SIZE, PAGE_SIZE)
        return (pltpu.make_async_copy(clat_hbm.at[layer, page], latbuf.at[sl, rows], sem.at[0, sl]),
                pltpu.make_async_copy(cropet_hbm.at[layer, page], ropebuf.at[sl, p], sem.at[1, sl]))

    def start_all(bb, sl):
        def body(p, carry):
            for cp in page_copies(sl, p, pt_ref[bb, p]):
                cp.start()
            return carry
        lax.fori_loop(0, n_pages, body, 0)

    @pl.when(g == 0)
    def _():
        start_all(0, 0)
        latb16[1] = jnp.zeros((past, KV_LORA), BF16)
        p_sc[1] = jnp.zeros((N_HEADS, past), BF16)
        tail_sc[1] = jnp.zeros((N_HEADS, KV_LORA), F32)
        den_sc[1] = jnp.ones((N_HEADS, 1), F32)

    @pl.when(g + 1 < nb)
    def _():
        start_all(g + 1, prev)

    @pl.when(g < nb)
    def _():
        for p in range(n_pages):
            for cp in page_copies(slot, p, 0):
                cp.wait()

    ql = qlat_ref[0]
    qr = qrope_ref[0]
    latn = latn_ref[0]
    ropen = ropen_ref[0]
    latb16[slot] = latbuf[slot].astype(BF16)
    s_rope = jnp.concatenate([_dot(qr, ropebuf[slot, p].astype(BF16)) for p in range(n_pages)], axis=1)
    s = _dot_nt(ql, latb16[slot]) + s_rope
    s_new = (jnp.sum(ql.astype(F32) * latn, -1, keepdims=True)
             + jnp.sum(qr.astype(F32) * ropen, -1, keepdims=True))
    m = jnp.maximum(jnp.max(s, -1, keepdims=True), s_new)
    p = jnp.exp2(s - m)
    p_new = jnp.exp2(s_new - m)

    o_ref[0] = (_dot(p_sc[prev], latb16[prev]) + tail_sc[prev]) / den_sc[prev]

    p_sc[slot] = p.astype(BF16)
    tail_sc[slot] = p_new * latn
    den_sc[slot] = jnp.sum(p, -1, keepdims=True) + p_new


def _decode_call(page_table, qlat, qrope, lat_new, rope_new, cache_lat, cache_rope, layer):
    bsz, n_pages = page_table.shape
    past = n_pages * PAGE_SIZE
    cur = lambda s1, s2: pl.BlockSpec((1, s1, s2), lambda g, pt: (jnp.minimum(g, bsz - 1), 0, 0))
    grid_spec = pltpu.PrefetchScalarGridSpec(
        num_scalar_prefetch=1,
        grid=(bsz + 1,),
        in_specs=[cur(N_HEADS, KV_LORA), cur(N_HEADS, QK_ROPE), cur(1, KV_LORA), cur(1, QK_ROPE),
                  pl.BlockSpec(memory_space=pl.ANY), pl.BlockSpec(memory_space=pl.ANY)],
        out_specs=pl.BlockSpec((1, N_HEADS, KV_LORA), lambda g, pt: (jnp.maximum(g - 1, 0), 0, 0)),
        scratch_shapes=[pltpu.VMEM((2, past, KV_LORA), F32), pltpu.VMEM((2, n_pages, QK_ROPE, PAGE_SIZE), F32),
                        pltpu.VMEM((2, past, KV_LORA), BF16), pltpu.VMEM((2, N_HEADS, past), BF16),
                        pltpu.VMEM((2, N_HEADS, KV_LORA), F32), pltpu.VMEM((2, N_HEADS, 1), F32),
                        pltpu.SemaphoreType.DMA((2, 2))],
    )
    return pl.pallas_call(
        functools.partial(_decode_kernel, layer=layer, n_pages=n_pages),
        grid_spec=grid_spec,
        out_shape=jax.ShapeDtypeStruct((bsz, N_HEADS, KV_LORA), F32),
        compiler_params=pltpu.CompilerParams(dimension_semantics=("arbitrary",), vmem_limit_bytes=VMEM_LIMIT),
        name="sample_decode_attention",
    )(page_table, qlat, qrope, lat_new, rope_new, cache_lat, cache_rope)


def _olat_kernel(olat_ref, wuvt_ref, o_ref):
    outs = [_dot_nt(olat_ref[:, h * KV_LORA:(h + 1) * KV_LORA].astype(BF16),
                    wuvt_ref[h * HEAD_PAD:(h + 1) * HEAD_PAD, :]) for h in range(N_HEADS)]
    for hp in range(N_HEADS // 2):
        pair = outs[2 * hp] + pltpu.roll(outs[2 * hp + 1], V_DIM, 1)
        o_ref[:, hp * HEAD_PAD:(hp + 1) * HEAD_PAD] = pair.astype(BF16)


def _olat_call(olat_flat, wuvt):
    n = olat_flat.shape[0]
    return pl.pallas_call(
        _olat_kernel,
        out_shape=jax.ShapeDtypeStruct((n, ATTN_WIDTH), BF16),
        name="sample_value_up",
    )(olat_flat, wuvt)


def _rope_tables(pos):
    freqs = ROPE_THETA ** (-jnp.arange(ROPE_HALF, dtype=F32) / ROPE_HALF)
    ang = pos.astype(F32)[:, None] * freqs[None, :]
    cos, sin = jnp.cos(ang), jnp.sin(ang)
    one = jnp.ones((pos.shape[0], QK_NOPE), F32)
    z64 = jnp.zeros((pos.shape[0], QK_NOPE), F32)
    z16 = jnp.zeros((pos.shape[0], ROPE_HALF), F32)
    c = jnp.concatenate([one, cos, cos, cos, cos], -1)
    sa = jnp.concatenate([z64, -sin, z16, -sin, z16], -1)
    sb = jnp.concatenate([z64, z16, sin, z16, sin], -1)
    return c, sa, sb


def _prep_layer(i, w_in, pool_w, pool_scale, pool_up, q_norm_g, w_uq, kv_norm_g, w_uk, w_uv, attn_up, w_out,
                ple_proj, ple_gate, ln_g, ln_b):
    w1t = jnp.swapaxes(w_in[i], 0, 1)
    wkr_t = w1t[O_KR:O_AG]
    wkr = jnp.concatenate([jnp.zeros((QK_NOPE, D_MODEL), F32), wkr_t, wkr_t], axis=0)
    pad_head = lambda a: jnp.pad(a, ((0, 0), (0, 0), (0, HEAD_PAD - a.shape[-1]))).reshape(a.shape[0], QK_PAD)
    wukt =jnp.pad(jnp.transpose(w_uk[i], (1, 2, 0)), ((0, 0), (0, HEAD_PAD - QK_NOPE), (0, 0)))
    wuvt = jnp.pad(jnp.transpose(w_uv[i], (1, 2, 0)), ((0, 0), (0, HEAD_PAD - V_DIM), (0, 0)))
    return {
        "w1t": w1t.astype(BF16),
        "wkr": wkr.astype(BF16),
        "pool_w": pool_w[i].astype(BF16),
        "pool_scale": pool_scale[i].reshape(1, POOL_WIDTH),
        "pool_up": pool_up[i].astype(BF16),
        "q_norm_g": q_norm_g[i].reshape(1, Q_LORA),
        "wuq": pad_head(w_uq[i].reshape(Q_LORA, N_HEADS, QK_HEAD)).astype(BF16),
        "kv_norm_g": kv_norm_g[i].reshape(1, KV_LORA),
        "wukt": wukt.reshape(QK_PAD, KV_LORA).astype(BF16),
        "wuvt": wuvt.reshape(QK_PAD, KV_LORA).astype(BF16),
        "attn_up": attn_up[i].astype(BF16),
        "w_out": w_out[i].astype(BF16),
        "ple_gate": ple_gate[i].astype(BF16),
        "ple_proj": ple_proj[i].astype(BF16),
        "ln_g": ln_g[i].reshape(1, D_MODEL),
        "ln_b": ln_b[i].reshape(1, D_MODEL),
    }


def kernel(x_prompt, x_sample, cache_kv_latent, cache_k_rope, state_pool, page_table, p_prompt, p_sample,
           ln_in_g, ln_in_b, w_in, pool_w, pool_scale, pool_up, q_norm_g, w_uq, kv_norm_g, w_uk, w_uv,
           attn_up, w_out, ple_proj, ple_gate, ln_g, ln_b):
    bsz, seq, _ = x_prompt.shape
    nsmp = x_sample.shape[0]
    n_pages = page_table.shape[1]
    past_len = n_pages * PAGE_SIZE
    n_tok = bsz * seq
    tm_pre, tm_post, bq, bk = 512, 512, 1024, 1024

    tabs_p = _rope_tables(jnp.arange(seq))
    tabs_s = _rope_tables(jnp.full((1,), past_len))
    ln_in = (ln_in_g.reshape(1, D_MODEL), ln_in_b.reshape(1, D_MODEL))
    cache_rope_t = jnp.swapaxes(cache_k_rope, 2, 3)

    hp = x_prompt.reshape(n_tok, D_MODEL)
    hs = x_sample.reshape(nsmp, D_MODEL)

    lat_p, rope_p, pool_p, lat_s, rope_s, pool_s = [], [], [], [], [], []
    for i in range(DEPTH):
        first = i == 0
        wl = _prep_layer(i, w_in, pool_w, pool_scale, pool_up, q_norm_g, w_uq, kv_norm_g, w_uk, w_uv, attn_up,
                         w_out, ple_proj, ple_gate, ln_g, ln_b)
        q, k, vt, lat, krope, gp, sga, ga, pstate = _pre_prompt_call(hp.reshape(bsz, seq, D_MODEL), ln_in, tabs_p,
                                                                     wl, tm_pre, first)
        o = _attn_call(q, k, vt, bq, bk)
        hp = _post_call(o.reshape(n_tok, ATTN_WIDTH), ga.reshape(n_tok, ATTN_WIDTH), gp.reshape(n_tok, D_MODEL),
                        sga.reshape(n_tok, D_MODEL), hp, p_prompt[i].reshape(n_tok, PLE_DIM), ln_in, wl, tm_post,
                        first)
        lat_p.append(lat)
        rope_p.append(krope)
        pool_p.append(pstate[:, 1:, :])
        qs, qlat, lat_n, rope_n, gp_s, sga_s, ga_s, pv_s = _pre_sample_call(
            hs, ln_in, jnp.swapaxes(state_pool[i], 0, 1), tabs_s, wl, first)
        qrope = qs.reshape(nsmp, N_HEADS, HEAD_PAD)[:, :, QK_NOPE:QK_HEAD]
        olat = _decode_call(page_table, qlat.reshape(nsmp, N_HEADS, KV_LORA), qrope,
                            lat_n.reshape(nsmp, 1, KV_LORA), rope_n.reshape(nsmp, 1, QK_ROPE),
                            cache_kv_latent, cache_rope_t, i)
        o_s = _olat_call(olat.reshape(nsmp, N_HEADS * KV_LORA), wl["wuvt"])
        hs = _post_call(o_s, ga_s, gp_s, sga_s, hs, p_sample[i].reshape(nsmp, PLE_DIM), ln_in, wl, nsmp, first)
        lat_s.append(lat_n.reshape(nsmp, 1, KV_LORA))
        rope_s.append(rope_n.reshape(nsmp, 1, QK_ROPE))
        pool_s.append(jnp.concatenate([state_pool[i][:, 1:, :], pv_s[:, None, :]], axis=1))

    return (hp.reshape(bsz, seq, D_MODEL), hs.reshape(nsmp, 1, D_MODEL), jnp.stack(lat_p), jnp.stack(rope_p),
            jnp.stack(pool_p), jnp.stack(lat_s), jnp.stack(rope_s), jnp.stack(pool_s))
```

```python
import functools
import math

import jax
import jax.numpy as jnp
import numpy as np
from jax import lax
from jax.experimental import pallas as pl
from jax.experimental.pallas import tpu as pltpu

F32 = jnp.float32
BF16 = jnp.bfloat16

D_MODEL = 1024
DEPTH = 2
PAGE_SIZE = 128
POOL_WINDOWS = (2, 4, 8, 16)
POOL_WIDTH = 512
POOL_GC = 128
POOL_WMAX = 16
POOL_BUF = POOL_WMAX - 1
N_HEADS = 8
QK_NOPE = 64
QK_ROPE = 32
ROPE_HALF = QK_ROPE // 2
QK_HEAD = QK_NOPE + QK_ROPE
V_DIM = 64
Q_LORA = 384
KV_LORA = 256
ATTN_WIDTH = N_HEADS * V_DIM
ROPE_THETA = 10000.0
PLE_DIM = 256
DN_ALPHA = (2 * DEPTH) ** 0.25
NORM_EPS = 1e-6
HEAD_PAD = 128
QK_PAD = N_HEADS * HEAD_PAD
Q_SCALE = (QK_HEAD ** -0.5) * math.log2(math.e)

O_PV, O_PG, O_CQ, O_CKV, O_KR, O_AG, O_MGP, O_MGA, O_END = 0, 512, 1024, 1408, 1664, 1696, 2208, 3232, 4256

VMEM_LIMIT = 56 * 1024 * 1024


def _const_spec(shape):
    nd = len(shape)
    return pl.BlockSpec(shape, lambda *_: (0,) * nd, pipeline_mode=pl.Buffered(1))


def _layer_norm(r, g, b):
    mu = jnp.mean(r, -1, keepdims=True)
    d = r - mu
    var = jnp.mean(d * d, -1, keepdims=True)
    return d * lax.rsqrt(var + NORM_EPS) * g + b


def _rms_norm(x, g):
    return x * lax.rsqrt(jnp.mean(x * x, -1, keepdims=True) + NORM_EPS) * g


def _silu(x):
    return x * jax.nn.sigmoid(x)


def _dot(a, b):
    return jnp.dot(a, b, preferred_element_type=F32)


def _dot_nt(a, b):
    return lax.dot_general(a, b, (((1,), (1,)), ((), ())), preferred_element_type=F32)


def _rope_block(blk, c, sa, sb):
    return blk * c + pltpu.roll(blk, HEAD_PAD - ROPE_HALF, 1) * sa + pltpu.roll(blk, ROPE_HALF, 1) * sb


def _pool_tail(pool_mean_minus_tok, pg, mgp, poolw_ref, pscale_ref, poolup_ref):
    ys = [_dot(m.astype(BF16), poolw_ref[g]) for g, m in enumerate(pool_mean_minus_tok)]
    y = jnp.concatenate(ys, axis=-1)
    yb = (y * pscale_ref[...] * _silu(pg)).astype(BF16)
    return jax.nn.sigmoid(mgp) * _dot(yb, poolup_ref[...])


def _q_heads(cq, qg_ref, wuq_ref, c, sa, sb):
    qf = _dot(_rms_norm(cq, qg_ref[...]).astype(BF16), wuq_ref[...])
    return [(_rope_block(qf[:, h * HEAD_PAD:(h + 1) * HEAD_PAD], c, sa, sb) * Q_SCALE).astype(BF16)
            for h in range(N_HEADS)]


def _rope_key(kr, c, sa, sb):
    return _rope_block(kr, c, sa, sb)


def _pre_prompt_kernel(x_ref, lnig_ref, lnib_ref, c_ref, sa_ref, sb_ref, w1t_ref, wkr_ref, poolw_ref, pscale_ref,
                       poolup_ref, qg_ref, wuq_ref, kvg_ref, wukt_ref, wuvt_ref,
                       q_ref, k_ref, vt_ref, lat_ref, krope_ref, gp_ref, sga_ref, ga_ref, pstate_ref,
                       ext_sc, *, tm, input_ln):
    si = pl.program_id(1)
    x = x_ref[0]
    if input_ln:
        x = _layer_norm(x, lnig_ref[...], lnib_ref[...])
    xb = x.astype(BF16)

    def seg(a, b):
        return _dot_nt(xb, w1t_ref[a:b, :])

    pv = seg(O_PV, O_PG)

    @pl.when(si == 0)
    def _():
        ext_sc[0:POOL_WMAX, :] = jnp.zeros((POOL_WMAX, POOL_WIDTH), F32)

    ext_sc[POOL_WMAX:POOL_WMAX + tm, :] = pv
    pos = si * tm + lax.broadcasted_iota(jnp.int32, (tm, 1), 0)
    means = []
    for g, w in enumerate(POOL_WINDOWS):
        c0 = g * POOL_GC
        acc = pv[:, c0:c0 + POOL_GC]
        for j in range(1, w):
            acc = acc + ext_sc[POOL_WMAX - j:POOL_WMAX - j + tm, c0:c0 + POOL_GC]
        inv_cnt = 1.0 / jnp.minimum(pos + 1, w).astype(F32)
        means.append(acc * inv_cnt - pv[:, c0:c0 + POOL_GC])
    tail = ext_sc[tm:tm + POOL_WMAX, :]
    pstate_ref[0] = tail
    ext_sc[0:POOL_WMAX, :] = tail
    gp_ref[0] = _pool_tail(means, seg(O_PG, O_CQ), seg(O_MGP, O_MGA), poolw_ref, pscale_ref,
                           poolup_ref).astype(BF16)
    sga_ref[0] = jax.nn.sigmoid(seg(O_MGA, O_END)).astype(BF16)
    ga_ref[0] = _silu(seg(O_AG, O_MGP)).astype(BF16)

    c, sa, sb = c_ref[...], sa_ref[...], sb_ref[...]
    for h, qh in enumerate(_q_heads(seg(O_CQ, O_CKV), qg_ref, wuq_ref, c, sa, sb)):
        q_ref[0, :, h * HEAD_PAD:(h + 1) * HEAD_PAD] = qh

    lat = _rms_norm(seg(O_CKV, O_KR), kvg_ref[...])
    lat_ref[0] = lat
    latb = lat.astype(BF16)
    krr = _rope_key(_dot_nt(xb, wkr_ref[...]), c, sa, sb)
    krope_ref[0] = krr[:, HEAD_PAD - QK_ROPE:]
    lane = lax.broadcasted_iota(jnp.int32, (tm, HEAD_PAD), 1)
    krk = jnp.where(lane < QK_HEAD, krr, 0.0)
    kn = _dot_nt(latb, wukt_ref[...])
    for h in range(N_HEADS):
        k_ref[0, :, h * HEAD_PAD:(h + 1) * HEAD_PAD] = (kn[:, h * HEAD_PAD:(h + 1) * HEAD_PAD] + krk).astype(BF16)
    vt = _dot_nt(wuvt_ref[...], latb)
    rowi = lax.broadcasted_iota(jnp.int32, (QK_PAD, tm), 0)
    vt_ref[0] = jnp.where((rowi & (HEAD_PAD - 1)) >= V_DIM, 1.0, vt).astype(BF16)


def _pre_prompt_call(x, ln_in, tabs, wl, tm, input_ln):
    bsz, seq, _ = x.shape
    row = lambda width: pl.BlockSpec((1, tm, width), lambda b, s: (b, s, 0))
    tab = pl.BlockSpec((tm, HEAD_PAD), lambda b, s: (s, 0))
    in_specs = [row(D_MODEL), _const_spec((1, D_MODEL)), _const_spec((1, D_MODEL)), tab, tab, tab,
                _const_spec((O_END, D_MODEL)), _const_spec((HEAD_PAD, D_MODEL)), _const_spec((4, POOL_GC, POOL_GC)),
                _const_spec((1, POOL_WIDTH)), _const_spec((POOL_WIDTH, D_MODEL)), _const_spec((1, Q_LORA)),
                _const_spec((Q_LORA, QK_PAD)), _const_spec((1, KV_LORA)), _const_spec((QK_PAD, KV_LORA)),
                _const_spec((QK_PAD, KV_LORA))]
    out_specs = [row(QK_PAD), row(QK_PAD), pl.BlockSpec((1, QK_PAD, tm), lambda b, s: (b, 0, s)),
                 row(KV_LORA), row(QK_ROPE), row(D_MODEL), row(D_MODEL),
                 row(ATTN_WIDTH), pl.BlockSpec((1, POOL_WMAX, POOL_WIDTH), lambda b, s: (b, 0, 0))]
    sds = jax.ShapeDtypeStruct
    out_shape = [sds((bsz, seq, QK_PAD), BF16), sds((bsz, seq, QK_PAD), BF16), sds((bsz, QK_PAD, seq), BF16),
                 sds((bsz, seq, KV_LORA), F32), sds((bsz, seq, QK_ROPE), F32), sds((bsz, seq, D_MODEL), BF16),
                 sds((bsz, seq, D_MODEL), BF16), sds((bsz, seq, ATTN_WIDTH), BF16),
                 sds((bsz, POOL_WMAX, POOL_WIDTH), F32)]
    return pl.pallas_call(
        functools.partial(_pre_prompt_kernel, tm=tm, input_ln=input_ln),
        grid=(bsz, seq // tm),
        in_specs=in_specs, out_specs=out_specs, out_shape=out_shape,
        scratch_shapes=[pltpu.VMEM((tm + POOL_WMAX, POOL_WIDTH), F32)],
        compiler_params=pltpu.CompilerParams(dimension_semantics=("arbitrary", "arbitrary"),
                                             vmem_limit_bytes=VMEM_LIMIT),
        name="prompt_pre",
    )(x, *ln_in, *tabs, wl["w1t"], wl["wkr"], wl["pool_w"], wl["pool_scale"], wl["pool_up"], wl["q_norm_g"], wl["wuq"],
      wl["kv_norm_g"], wl["wukt"], wl["wuvt"])


def _attn_kernel(qtile_ref, ktile_ref, q_ref, k_ref, vt_ref, o_ref, m_sc, acc_sc, ot_sc, *, bq, bk):
    t = pl.program_id(1)
    qi = qtile_ref[t]
    ki = ktile_ref[t]
    kper = bq // bk

    @pl.when(ki == 0)
    def _():
        m_sc[...] = jnp.full(m_sc.shape, -jnp.inf, F32)
        acc_sc[...] = jnp.zeros(acc_sc.shape, F32)

    def piece(qlo, qhi, khi, masked):
        qs = slice(qlo, qhi)
        if masked:
            key_pos = ki * bk + lax.broadcasted_iota(jnp.int32, (khi, qhi - qlo), 0)
            qry_pos = qi * bq + qlo + lax.broadcasted_iota(jnp.int32, (khi, qhi - qlo), 1)
            keep = key_pos <= qry_pos

        def scores(h):
            sl = slice(h * HEAD_PAD, (h + 1) * HEAD_PAD)
            return _dot_nt(k_ref[0, :khi, sl], q_ref[0, qs, sl]).astype(BF16)

        st_next = scores(0)
        for h in range(N_HEADS):
            sl = slice(h * HEAD_PAD, (h + 1) * HEAD_PAD)
            st = st_next
            if h + 1 < N_HEADS:
                st_next = scores(h + 1)
            if masked:
                st = jnp.where(keep, st, -jnp.inf)
            m_prev = m_sc[h, :, qs]
            m_new = jnp.maximum(m_prev, jnp.max(st, axis=0, keepdims=True).astype(F32))
            alpha = jnp.exp2(m_prev - m_new)
            pt = jnp.exp2(st - m_new.astype(BF16))
            acc_sc[h, :, qs] = alpha * acc_sc[h, :, qs] + _dot(vt_ref[0, sl, :khi], pt)
            m_sc[h, :, qs] = m_new

    @pl.when(ki < qi * kper)
    def _():
        piece(0, bq, bk, False)

    @pl.when(jnp.logical_and(ki >= qi * kper, ki < (qi + 1) * kper))
    def _():
        if bq == bk:
            piece(0, bq // 2, bk // 2, True)
            piece(bq // 2, bq, bk, True)
        else:
            piece(0, bq, bk, True)

    @pl.when(ki == (qi + 1) * kper - 1)
    def _():
        for h in range(N_HEADS):
            acc = acc_sc[h]
            ot_sc[h * V_DIM:(h + 1) * V_DIM, :] = acc[:V_DIM, :] / acc[V_DIM:V_DIM + 1, :]
        o_ref[0] = ot_sc[...].T.astype(BF16)


def _attn_call(q, k, vt, bq, bk):
    bsz, seq, _ = q.shape
    kper = bq // bk
    pairs = [(i, j) for i in range(seq // bq) for j in range((i + 1) * kper)]
    qtile = jnp.asarray(np.array([p[0] for p in pairs], np.int32))
    ktile = jnp.asarray(np.array([p[1] for p in pairs], np.int32))
    grid_spec = pltpu.PrefetchScalarGridSpec(
        num_scalar_prefetch=2,
        grid=(bsz, len(pairs)),
        in_specs=[pl.BlockSpec((1, bq, QK_PAD), lambda b, t, qt, kt: (b, qt[t], 0)),
                  pl.BlockSpec((1, bk, QK_PAD), lambda b, t, qt, kt: (b, kt[t], 0)),
                  pl.BlockSpec((1, QK_PAD, bk), lambda b, t, qt, kt: (b, 0, kt[t]))],
        out_specs=pl.BlockSpec((1, bq, ATTN_WIDTH), lambda b, t, qt, kt: (b, qt[t], 0)),
        scratch_shapes=[pltpu.VMEM((N_HEADS, 1, bq), F32), pltpu.VMEM((N_HEADS, HEAD_PAD, bq), F32),
                        pltpu.VMEM((ATTN_WIDTH, bq), F32)],
    )
    return pl.pallas_call(
        functools.partial(_attn_kernel, bq=bq, bk=bk),
        grid_spec=grid_spec,
        out_shape=jax.ShapeDtypeStruct((bsz, seq, ATTN_WIDTH), BF16),
        compiler_params=pltpu.CompilerParams(dimension_semantics=("arbitrary", "arbitrary"),
                                             vmem_limit_bytes=VMEM_LIMIT),
        name="prompt_attention",
    )(qtile, ktile, q, k, vt)


def _post_kernel(o_ref, ga_ref, gp_ref, sga_ref, x_ref, p_ref, lnig_ref, lnib_ref, attnup_ref, wout_ref, pgate_ref,
                 pproj_ref, lng_ref, lnb_ref, y_ref, *, input_ln):
    og = (o_ref[...].astype(F32) * ga_ref[...].astype(F32)).astype(BF16)
    b_attn = _dot(og, attnup_ref[...])
    mix = gp_ref[...].astype(F32) + sga_ref[...].astype(F32) * b_attn
    sub = _dot(mix.astype(BF16), wout_ref[...])
    x = x_ref[...]
    if input_ln:
        x = _layer_norm(x, lnig_ref[...], lnib_ref[...])
    ple = jax.nn.sigmoid(_dot(x.astype(BF16), pgate_ref[...])) * _dot(p_ref[0].astype(BF16), pproj_ref[...])
    y_ref[...] = _layer_norm(DN_ALPHA * x + sub + ple, lng_ref[...], lnb_ref[...])


def _post_call(o, ga, gp, sga, x, p_all, layer, ln_in, wl, tm, input_ln):
    n = x.shape[0]
    row = lambda width: pl.BlockSpec((tm, width), lambda i: (i, 0))
    return pl.pallas_call(
        functools.partial(_post_kernel, input_ln=input_ln),
        grid=(n // tm,),
        in_specs=[row(ATTN_WIDTH), row(ATTN_WIDTH), row(D_MODEL), row(D_MODEL), row(D_MODEL),
                  pl.BlockSpec((1, tm, PLE_DIM), lambda i: (layer, i, 0)),
                  _const_spec((1, D_MODEL)), _const_spec((1, D_MODEL)),
                  _const_spec((ATTN_WIDTH, D_MODEL)), _const_spec((D_MODEL, D_MODEL)),
                  _const_spec((D_MODEL, D_MODEL)), _const_spec((PLE_DIM, D_MODEL)),
                  _const_spec((1, D_MODEL)), _const_spec((1, D_MODEL))],
        out_specs=row(D_MODEL),
        out_shape=jax.ShapeDtypeStruct((n, D_MODEL), F32),
        compiler_params=pltpu.CompilerParams(dimension_semantics=("arbitrary",), vmem_limit_bytes=VMEM_LIMIT),
        name="layer_post",
    )(o, ga, gp, sga, x, p_all, *ln_in, wl["attn_up"], wl["w_out"], wl["ple_gate"], wl["ple_proj"], wl["ln_g"],
      wl["ln_b"])


def _pre_sample_kernel(x_ref, lnig_ref, lnib_ref, state_ref, c_ref, sa_ref, sb_ref, w1t_ref, wkr_ref, poolw_ref,
                       pscale_ref, poolup_ref, qg_ref, wuq_ref, kvg_ref, wukt_ref,
                       q_ref, qlat_ref, lat_ref, krope_ref, gp_ref, sga_ref, ga_ref, pv_ref, *, input_ln):
    x = x_ref[...]
    if input_ln:
        x = _layer_norm(x, lnig_ref[...], lnib_ref[...])
    xb = x.astype(BF16)

    def seg(a, b):
        return _dot_nt(xb, w1t_ref[a:b, :])

    pv = seg(O_PV, O_PG)
    pv_ref[...] = pv
    means = []
    for g, w in enumerate(POOL_WINDOWS):
        c0 = g * POOL_GC
        acc = pv[:, c0:c0 + POOL_GC]
        for j in range(1, w):
            acc = acc + state_ref[POOL_BUF - j, :, c0:c0 + POOL_GC]
        means.append(acc / float(w) - pv[:, c0:c0 + POOL_GC])
    gp_ref[...] = _pool_tail(means, seg(O_PG, O_CQ), seg(O_MGP, O_MGA), poolw_ref, pscale_ref,
                             poolup_ref).astype(BF16)
    sga_ref[...] = jax.nn.sigmoid(seg(O_MGA, O_END)).astype(BF16)
    ga_ref[...] = _silu(seg(O_AG, O_MGP)).astype(BF16)

    c, sa, sb = c_ref[...], sa_ref[...], sb_ref[...]
    qh = _q_heads(seg(O_CQ, O_CKV), qg_ref, wuq_ref, c, sa, sb)
    qf = jnp.concatenate(qh, axis=-1)
    q_ref[...] = qf
    for h in range(N_HEADS):
        qlat_ref[:, h * KV_LORA:(h + 1) * KV_LORA] = _dot(
            qh[h], wukt_ref[h * HEAD_PAD:(h + 1) * HEAD_PAD, :]).astype(BF16)
    lat_ref[...] = _rms_norm(seg(O_CKV, O_KR), kvg_ref[...])
    krope_ref[...] = _rope_key(_dot_nt(xb, wkr_ref[...]), c, sa, sb)[:, HEAD_PAD - QK_ROPE:]


def _pre_sample_call(x, ln_in, state_t, tabs, wl, input_ln):
    n = x.shape[0]
    sds = jax.ShapeDtypeStruct
    out_shape = [sds((n, QK_PAD), BF16), sds((n, N_HEADS * KV_LORA), BF16), sds((n, KV_LORA), F32),
                 sds((n, QK_ROPE), F32), sds((n, D_MODEL), BF16), sds((n, D_MODEL), BF16),
                 sds((n, ATTN_WIDTH), BF16), sds((n, POOL_WIDTH), F32)]
    return pl.pallas_call(
        functools.partial(_pre_sample_kernel, input_ln=input_ln),
        out_shape=out_shape,
        compiler_params=pltpu.CompilerParams(vmem_limit_bytes=VMEM_LIMIT),
        name="sample_pre",
    )(x, *ln_in, state_t, *tabs, wl["w1t"], wl["wkr"], wl["pool_w"], wl["pool_scale"], wl["pool_up"], wl["q_norm_g"], wl["wuq"],
      wl["kv_norm_g"], wl["wukt"])


def _decode_kernel(pt_ref, qlat_ref, qrope_ref, latn_ref, ropen_ref, clat_hbm, cropet_hbm, o_ref,
                   latbuf, ropebuf, latb16, p_sc, tail_sc, den_sc, sem, *, layer, n_pages):
    g = pl.program_id(0)
    nb = pl.num_programs(0) - 1
    slot = g % 2
    prev = 1 - slot
    past = n_pages * PAGE_SIZE

    def page_copies(sl, p, page):
        rows = pl.ds(p * PAGE_SIZE, PAGE_SIZE)
        return (pltpu.make_async_copy(clat_hbm.at[layer, page], latbuf.at[sl, rows], sem.at[0, sl]),
                pltpu.make_async_copy(cropet_hbm.at[layer, page], ropebuf.at[sl, p], sem.at[1, sl]))

    def start_all(bb, sl):
        def body(p, carry):
            for cp in page_copies(sl, p, pt_ref[bb, p]):
                cp.start()
            return carry
        lax.fori_loop(0, n_pages, body, 0)

    @pl.when(g == 0)
    def _():
        start_all(0, 0)
        latb16[1] = jnp.zeros((past, KV_LORA), BF16)
        p_sc[1] = jnp.zeros((N_HEADS, past), BF16)
        tail_sc[1] = jnp.zeros((N_HEADS, KV_LORA), F32)
        den_sc[1] = jnp.ones((N_HEADS, 1), F32)

    @pl.when(g + 1 < nb)
    def _():
        start_all(g + 1, prev)

    @pl.when(g < nb)
    def _():
        for p in range(n_pages):
            for cp in page_copies(slot, p, 0):
                cp.wait()

    ql = qlat_ref[0]
    qr = qrope_ref[0]
    latn = latn_ref[0]
    ropen = ropen_ref[0]
    latb16[slot] = latbuf[slot].astype(BF16)
    s_rope = jnp.concatenate([_dot(qr, ropebuf[slot, p].astype(BF16)) for p in range(n_pages)], axis=1)
    s = _dot_nt(ql, latb16[slot]) + s_rope
    s_new = (jnp.sum(ql.astype(F32) * latn, -1, keepdims=True)
             + jnp.sum(qr.astype(F32) * ropen, -1, keepdims=True))
    m = jnp.maximum(jnp.max(s, -1, keepdims=True), s_new)
    p = jnp.exp2(s - m)
    p_new = jnp.exp2(s_new - m)

    o_ref[0] = (_dot(p_sc[prev], latb16[prev]) + tail_sc[prev]) / den_sc[prev]

    p_sc[slot] = p.astype(BF16)
    tail_sc[slot] = p_new * latn
    den_sc[slot] = jnp.sum(p, -1, keepdims=True) + p_new


def _decode_call(page_table, qlat, qrope, lat_new, rope_new, cache_lat, cache_rope, layer):
    bsz, n_pages = page_table.shape
    past = n_pages * PAGE_SIZE
    cur = lambda s1, s2: pl.BlockSpec((1, s1, s2), lambda g, pt: (jnp.minimum(g, bsz - 1), 0, 0))
    grid_spec = pltpu.PrefetchScalarGridSpec(
        num_scalar_prefetch=1,
        grid=(bsz + 1,),
        in_specs=[cur(N_HEADS, KV_LORA), cur(N_HEADS, QK_ROPE), cur(1, KV_LORA), cur(1, QK_ROPE),
                  pl.BlockSpec(memory_space=pl.ANY), pl.BlockSpec(memory_space=pl.ANY)],
        out_specs=pl.BlockSpec((1, N_HEADS, KV_LORA), lambda g, pt: (jnp.maximum(g - 1, 0), 0, 0)),
        scratch_shapes=[pltpu.VMEM((2, past, KV_LORA), F32), pltpu.VMEM((2, n_pages, QK_ROPE, PAGE_SIZE), F32),
                        pltpu.VMEM((2, past, KV_LORA), BF16), pltpu.VMEM((2, N_HEADS, past), BF16),
                        pltpu.VMEM((2, N_HEADS, KV_LORA), F32), pltpu.VMEM((2, N_HEADS, 1), F32),
                        pltpu.SemaphoreType.DMA((2, 2))],
    )
    return pl.pallas_call(
        functools.partial(_decode_kernel, layer=layer, n_pages=n_pages),
        grid_spec=grid_spec,
        out_shape=jax.ShapeDtypeStruct((bsz, N_HEADS, KV_LORA), F32),
        compiler_params=pltpu.CompilerParams(dimension_semantics=("arbitrary",), vmem_limit_bytes=VMEM_LIMIT),
        name="sample_decode_attention",
    )(page_table, qlat, qrope, lat_new, rope_new, cache_lat, cache_rope)


def _olat_kernel(olat_ref, wuvt_ref, o_ref):
    outs = [_dot_nt(olat_ref[:, h * KV_LORA:(h + 1) * KV_LORA].astype(BF16),
                    wuvt_ref[h * HEAD_PAD:(h + 1) * HEAD_PAD, :]) for h in range(N_HEADS)]
    for hp in range(N_HEADS // 2):
        pair = outs[2 * hp] + pltpu.roll(outs[2 * hp + 1], V_DIM, 1)
        o_ref[:, hp * HEAD_PAD:(hp + 1) * HEAD_PAD] = pair.astype(BF16)


def _olat_call(olat_flat, wuvt):
    n = olat_flat.shape[0]
    return pl.pallas_call(
        _olat_kernel,
        out_shape=jax.ShapeDtypeStruct((n, ATTN_WIDTH), BF16),
        name="sample_value_up",
    )(olat_flat, wuvt)


def _rope_tables(pos):
    lane = jnp.arange(HEAD_PAD)
    freqs = ROPE_THETA ** (-(lane % ROPE_HALF).astype(F32) / ROPE_HALF)
    ang = pos.astype(F32)[:, None] * freqs[None, :]
    cos, sin = jnp.cos(ang), jnp.sin(ang)
    is_rope = lane >= QK_NOPE
    first_half = is_rope & ((lane // ROPE_HALF) % 2 == 0)
    second_half = is_rope & ((lane // ROPE_HALF) % 2 == 1)
    c = jnp.where(is_rope[None, :], cos, 1.0)
    sa = jnp.where(first_half[None, :], -sin, 0.0)
    sb = jnp.where(second_half[None, :], sin, 0.0)
    return c, sa, sb


def _prep_layer(i, w_in, pool_w, pool_scale, pool_up, q_norm_g, w_uq, kv_norm_g, w_uk, w_uv, attn_up, w_out,
                ple_proj, ple_gate, ln_g, ln_b):
    w1t = jnp.swapaxes(w_in[i], 0, 1)
    wkr_t = w1t[O_KR:O_AG]
    wkr = jnp.concatenate([jnp.zeros((QK_NOPE, D_MODEL), F32), wkr_t, wkr_t], axis=0)
    pad_head = lambda a: jnp.pad(a, ((0, 0), (0, 0), (0, HEAD_PAD - a.shape[-1]))).reshape(a.shape[0], QK_PAD)
    wukt =jnp.pad(jnp.transpose(w_uk[i], (1, 2, 0)), ((0, 0), (0, HEAD_PAD - QK_NOPE), (0, 0)))
    wuvt = jnp.pad(jnp.transpose(w_uv[i], (1, 2, 0)), ((0, 0), (0, HEAD_PAD - V_DIM), (0, 0)))
    return {
        "w1t": w1t.astype(BF16),
        "wkr": wkr.astype(BF16),
        "pool_w": pool_w[i].astype(BF16),
        "pool_scale": pool_scale[i].reshape(1, POOL_WIDTH),
        "pool_up": pool_up[i].astype(BF16),
        "q_norm_g": q_norm_g[i].reshape(1, Q_LORA),
        "wuq": pad_head(w_uq[i].reshape(Q_LORA, N_HEADS, QK_HEAD)).astype(BF16),
        "kv_norm_g": kv_norm_g[i].reshape(1, KV_LORA),
        "wukt": wukt.reshape(QK_PAD, KV_LORA).astype(BF16),
        "wuvt": wuvt.reshape(QK_PAD, KV_LORA).astype(BF16),
        "attn_up": attn_up[i].astype(BF16),
        "w_out": w_out[i].astype(BF16),
        "ple_gate": ple_gate[i].astype(BF16),
        "ple_proj": ple_proj[i].astype(BF16),
        "ln_g": ln_g[i].reshape(1, D_MODEL),
        "ln_b": ln_b[i].reshape(1, D_MODEL),
    }


def kernel(x_prompt, x_sample, cache_kv_latent, cache_k_rope, state_pool, page_table, p_prompt, p_sample,
           ln_in_g, ln_in_b, w_in, pool_w, pool_scale, pool_up, q_norm_g, w_uq, kv_norm_g, w_uk, w_uv,
           attn_up, w_out, ple_proj, ple_gate, ln_g, ln_b):
    bsz, seq, _ = x_prompt.shape
    nsmp = x_sample.shape[0]
    n_pages = page_table.shape[1]
    past_len = n_pages * PAGE_SIZE
    n_tok = bsz * seq
    tm_pre, tm_post, bq, bk = 512, 512, 1024, 1024

    tabs_p = _rope_tables(jnp.arange(seq))
    tabs_s = _rope_tables(jnp.full((1,), past_len))
    ln_in = (ln_in_g.reshape(1, D_MODEL), ln_in_b.reshape(1, D_MODEL))
    cache_rope_t = jnp.swapaxes(cache_k_rope, 2, 3)

    hp = x_prompt.reshape(n_tok, D_MODEL)
    hs = x_sample.reshape(nsmp, D_MODEL)

    lat_p, rope_p, pool_p, lat_s, rope_s, pool_s = [], [], [], [], [], []
    for i in range(DEPTH):
        first = i == 0
        wl = _prep_layer(i, w_in, pool_w, pool_scale, pool_up, q_norm_g, w_uq, kv_norm_g, w_uk, w_uv, attn_up,
                         w_out, ple_proj, ple_gate, ln_g, ln_b)
        q, k, vt, lat, krope, gp, sga, ga, pstate = _pre_prompt_call(hp.reshape(bsz, seq, D_MODEL), ln_in, tabs_p,
                                                                     wl, tm_pre, first)
        o = _attn_call(q, k, vt, bq, bk)
        hp = _post_call(o.reshape(n_tok, ATTN_WIDTH), ga.reshape(n_tok, ATTN_WIDTH), gp.reshape(n_tok, D_MODEL),
                        sga.reshape(n_tok, D_MODEL), hp, p_prompt.reshape(DEPTH, n_tok, PLE_DIM), i, ln_in, wl, tm_post,
                        first)
        lat_p.append(lat)
        rope_p.append(krope)
        pool_p.append(pstate[:, 1:, :])
        qs, qlat, lat_n, rope_n, gp_s, sga_s, ga_s, pv_s = _pre_sample_call(
            hs, ln_in, jnp.swapaxes(state_pool[i], 0, 1), tabs_s, wl, first)
        qrope = qs.reshape(nsmp, N_HEADS, HEAD_PAD)[:, :, QK_NOPE:QK_HEAD]
        olat = _decode_call(page_table, qlat.reshape(nsmp, N_HEADS, KV_LORA), qrope,
                            lat_n.reshape(nsmp, 1, KV_LORA), rope_n.reshape(nsmp, 1, QK_ROPE),
                            cache_kv_latent, cache_rope_t, i)
        o_s = _olat_call(olat.reshape(nsmp, N_HEADS * KV_LORA), wl["wuvt"])
        hs = _post_call(o_s, ga_s, gp_s, sga_s, hs, p_sample.reshape(DEPTH, nsmp, PLE_DIM), i, ln_in, wl, nsmp,
                        first)
        lat_s.append(lat_n.reshape(nsmp, 1, KV_LORA))
        rope_s.append(rope_n.reshape(nsmp, 1, QK_ROPE))
        pool_s.append(jnp.concatenate([state_pool[i][:, 1:, :], pv_s[:, None, :]], axis=1))

    return (hp.reshape(bsz, seq, D_MODEL), hs.reshape(nsmp, 1, D_MODEL), jnp.stack(lat_p), jnp.stack(rope_p),
            jnp.stack(pool_p), jnp.stack(lat_s), jnp.stack(rope_s), jnp.stack(pool_s))
```

```python
import functools
import math

import jax
import jax.numpy as jnp
import numpy as np
from jax import lax
from jax.experimental import pallas as pl
from jax.experimental.pallas import tpu as pltpu

F32 = jnp.float32
BF16 = jnp.bfloat16

D_MODEL = 1024
DEPTH = 2
PAGE_SIZE = 128
POOL_WINDOWS = (2, 4, 8, 16)
POOL_WIDTH = 512
POOL_GC = 128
POOL_WMAX = 16
POOL_BUF = POOL_WMAX - 1
N_HEADS = 8
QK_NOPE = 64
QK_ROPE = 32
ROPE_HALF = QK_ROPE // 2
QK_HEAD = QK_NOPE + QK_ROPE
V_DIM = 64
Q_LORA = 384
KV_LORA = 256
ATTN_WIDTH = N_HEADS * V_DIM
ROPE_THETA = 10000.0
PLE_DIM = 256
DN_ALPHA = (2 * DEPTH) ** 0.25
NORM_EPS = 1e-6
HEAD_PAD = 128
QK_PAD = N_HEADS * HEAD_PAD
Q_SCALE = (QK_HEAD ** -0.5) * math.log2(math.e)

O_PV, O_PG, O_CQ, O_CKV, O_KR, O_AG, O_MGP, O_MGA, O_END = 0, 512, 1024, 1408, 1664, 1696, 2208, 3232, 4256

VMEM_LIMIT = 56 * 1024 * 1024


def _const_spec(shape):
    nd = len(shape)
    return pl.BlockSpec(shape, lambda *_: (0,) * nd, pipeline_mode=pl.Buffered(1))


def _layer_norm(r, g, b):
    mu = jnp.mean(r, -1, keepdims=True)
    d = r - mu
    var = jnp.mean(d * d, -1, keepdims=True)
    return d * lax.rsqrt(var + NORM_EPS) * g + b


def _rms_norm(x, g):
    return x * lax.rsqrt(jnp.mean(x * x, -1, keepdims=True) + NORM_EPS) * g


def _silu(x):
    return x * jax.nn.sigmoid(x)


def _dot(a, b):
    return jnp.dot(a, b, preferred_element_type=F32)


def _dot_nt(a, b):
    return lax.dot_general(a, b, (((1,), (1,)), ((), ())), preferred_element_type=F32)


def _rope_block(blk, c, sa, sb):
    return blk * c + pltpu.roll(blk, HEAD_PAD - ROPE_HALF, 1) * sa + pltpu.roll(blk, ROPE_HALF, 1) * sb


def _pool_tail(pool_mean_minus_tok, pg, mgp, poolw_ref, pscale_ref, poolup_ref):
    ys = [_dot(m.astype(BF16), poolw_ref[g]) for g, m in enumerate(pool_mean_minus_tok)]
    y = jnp.concatenate(ys, axis=-1)
    yb = (y * pscale_ref[...] * _silu(pg)).astype(BF16)
    return jax.nn.sigmoid(mgp) * _dot(yb, poolup_ref[...])


def _q_heads(cq, qg_ref, wuq_ref, c, sa, sb):
    qf = _dot(_rms_norm(cq, qg_ref[...]).astype(BF16), wuq_ref[...])
    return [(_rope_block(qf[:, h * HEAD_PAD:(h + 1) * HEAD_PAD], c, sa, sb) * Q_SCALE).astype(BF16)
            for h in range(N_HEADS)]


def _rope_key(kr, c, sa, sb):
    return _rope_block(kr, c, sa, sb)


def _pre_prompt_kernel(x_ref, lnig_ref, lnib_ref, c_ref, sa_ref, sb_ref, w1t_ref, wkr_ref, poolw_ref, pscale_ref,
                       poolup_ref, qg_ref, wuq_ref, kvg_ref, wukt_ref, wuvt_ref,
                       q_ref, k_ref, vt_ref, lat_ref, krope_ref, gp_ref, sga_ref, ga_ref, pstate_ref,
                       ext_sc, *, tm, input_ln):
    si = pl.program_id(1)
    x = x_ref[0]
    if input_ln:
        x = _layer_norm(x, lnig_ref[...], lnib_ref[...])
    xb = x.astype(BF16)

    def seg(a, b):
        return _dot_nt(xb, w1t_ref[a:b, :])

    pv = seg(O_PV, O_PG)

    @pl.when(si == 0)
    def _():
        ext_sc[0:POOL_WMAX, :] = jnp.zeros((POOL_WMAX, POOL_WIDTH), F32)

    ext_sc[POOL_WMAX:POOL_WMAX + tm, :] = pv
    pos = si * tm + lax.broadcasted_iota(jnp.int32, (tm, 1), 0)
    means = []
    for g, w in enumerate(POOL_WINDOWS):
        c0 = g * POOL_GC
        acc = pv[:, c0:c0 + POOL_GC]
        for j in range(1, w):
            acc = acc + ext_sc[POOL_WMAX - j:POOL_WMAX - j + tm, c0:c0 + POOL_GC]
        inv_cnt = 1.0 / jnp.minimum(pos + 1, w).astype(F32)
        means.append(acc * inv_cnt - pv[:, c0:c0 + POOL_GC])
    tail = ext_sc[tm:tm + POOL_WMAX, :]
    pstate_ref[0] = tail
    ext_sc[0:POOL_WMAX, :] = tail
    gp_ref[0] = _pool_tail(means, seg(O_PG, O_CQ), seg(O_MGP, O_MGA), poolw_ref, pscale_ref,
                           poolup_ref).astype(BF16)
    sga_ref[0] = jax.nn.sigmoid(seg(O_MGA, O_END)).astype(BF16)
    ga_ref[0] = _silu(seg(O_AG, O_MGP)).astype(BF16)

    c, sa, sb = c_ref[...], sa_ref[...], sb_ref[...]
    for h, qh in enumerate(_q_heads(seg(O_CQ, O_CKV), qg_ref, wuq_ref, c, sa, sb)):
        q_ref[0, :, h * HEAD_PAD:(h + 1) * HEAD_PAD] = qh

    lat = _rms_norm(seg(O_CKV, O_KR), kvg_ref[...])
    lat_ref[0] = lat
    latb = lat.astype(BF16)
    krr = _rope_key(_dot_nt(xb, wkr_ref[...]), c, sa, sb)
    krope_ref[0] = krr[:, HEAD_PAD - QK_ROPE:]
    lane = lax.broadcasted_iota(jnp.int32, (tm, HEAD_PAD), 1)
    krk = jnp.where(lane < QK_HEAD, krr, 0.0)
    kn = _dot_nt(latb, wukt_ref[...])
    for h in range(N_HEADS):
        k_ref[0, :, h * HEAD_PAD:(h + 1) * HEAD_PAD] = (kn[:, h * HEAD_PAD:(h + 1) * HEAD_PAD] + krk).astype(BF16)
    vt = _dot_nt(wuvt_ref[...], latb)
    rowi = lax.broadcasted_iota(jnp.int32, (QK_PAD, tm), 0)
    vt_ref[0] = jnp.where((rowi & (HEAD_PAD - 1)) >= V_DIM, 1.0, vt).astype(BF16)


def _pre_prompt_call(x, ln_in, tabs, wl, tm, input_ln):
    bsz, seq, _ = x.shape
    row = lambda width: pl.BlockSpec((1, tm, width), lambda b, s: (b, s, 0))
    tab = pl.BlockSpec((tm, HEAD_PAD), lambda b, s: (s, 0))
    in_specs = [row(D_MODEL), _const_spec((1, D_MODEL)), _const_spec((1, D_MODEL)), tab, tab, tab,
                _const_spec((O_END, D_MODEL)), _const_spec((HEAD_PAD, D_MODEL)), _const_spec((4, POOL_GC, POOL_GC)),
                _const_spec((1, POOL_WIDTH)), _const_spec((POOL_WIDTH, D_MODEL)), _const_spec((1, Q_LORA)),
                _const_spec((Q_LORA, QK_PAD)), _const_spec((1, KV_LORA)), _const_spec((QK_PAD, KV_LORA)),
                _const_spec((QK_PAD, KV_LORA))]
    out_specs = [row(QK_PAD), row(QK_PAD), pl.BlockSpec((1, QK_PAD, tm), lambda b, s: (b, 0, s)),
                 row(KV_LORA), row(QK_ROPE), row(D_MODEL), row(D_MODEL),
                 row(ATTN_WIDTH), pl.BlockSpec((1, POOL_WMAX, POOL_WIDTH), lambda b, s: (b, 0, 0))]
    sds = jax.ShapeDtypeStruct
    out_shape = [sds((bsz, seq, QK_PAD), BF16), sds((bsz, seq, QK_PAD), BF16), sds((bsz, QK_PAD, seq), BF16),
                 sds((bsz, seq, KV_LORA), F32), sds((bsz, seq, QK_ROPE), F32), sds((bsz, seq, D_MODEL), BF16),
                 sds((bsz, seq, D_MODEL), BF16), sds((bsz, seq, ATTN_WIDTH), BF16),
                 sds((bsz, POOL_WMAX, POOL_WIDTH), F32)]
    return pl.pallas_call(
        functools.partial(_pre_prompt_kernel, tm=tm, input_ln=input_ln),
        grid=(bsz, seq // tm),
        in_specs=in_specs, out_specs=out_specs, out_shape=out_shape,
        scratch_shapes=[pltpu.VMEM((tm + POOL_WMAX, POOL_WIDTH), F32)],
        compiler_params=pltpu.CompilerParams(dimension_semantics=("arbitrary", "arbitrary"),
                                             vmem_limit_bytes=VMEM_LIMIT),
        name="prompt_pre",
    )(x, *ln_in, *tabs, wl["w1t"], wl["wkr"], wl["pool_w"], wl["pool_scale"], wl["pool_up"], wl["q_norm_g"], wl["wuq"],
      wl["kv_norm_g"], wl["wukt"], wl["wuvt"])


DIAG_SPLIT = 4


def _attn_kernel(qtile_ref, ktile_ref, q_ref, k_ref, vt_ref, o_ref, m_sc, acc_sc, ot_sc, *, bq, bk):
    t = pl.program_id(1)
    qi = qtile_ref[t]
    ki = ktile_ref[t]
    kper = bq // bk

    @pl.when(ki == 0)
    def _():
        m_sc[...] = jnp.full(m_sc.shape, -jnp.inf, F32)
        acc_sc[...] = jnp.zeros(acc_sc.shape, F32)

    def piece(qlo, qhi, khi, masked):
        qs = slice(qlo, qhi)
        if masked:
            key_pos = ki * bk + lax.broadcasted_iota(jnp.int32, (khi, qhi - qlo), 0)
            qry_pos = qi * bq + qlo + lax.broadcasted_iota(jnp.int32, (khi, qhi - qlo), 1)
            bias = jnp.where(key_pos <= qry_pos, 0.0, -jnp.inf).astype(BF16)

        def scores(h):
            sl = slice(h * HEAD_PAD, (h + 1) * HEAD_PAD)
            return _dot_nt(k_ref[0, :khi, sl], q_ref[0, qs, sl]).astype(BF16)

        st_next = scores(0)
        for h in range(N_HEADS):
            sl = slice(h * HEAD_PAD, (h + 1) * HEAD_PAD)
            st = st_next
            if h + 1 < N_HEADS:
                st_next = scores(h + 1)
            if masked:
                st = st + bias
            m_prev = m_sc[h, :, qs]
            m_new = jnp.maximum(m_prev, jnp.max(st, axis=0, keepdims=True).astype(F32))
            alpha = jnp.exp2(m_prev - m_new)
            pt = jnp.exp2(st - m_new.astype(BF16))
            acc_sc[h, :, qs] = alpha * acc_sc[h, :, qs] + _dot(vt_ref[0, sl, :khi], pt)
            m_sc[h, :, qs] = m_new

    @pl.when(ki < qi * kper)
    def _():
        piece(0, bq, bk, False)

    @pl.when(jnp.logical_and(ki >= qi * kper, ki < (qi + 1) * kper))
    def _():
        if bq == bk:
            for i in range(DIAG_SPLIT):
                piece(i * bq // DIAG_SPLIT, (i + 1) * bq // DIAG_SPLIT, (i + 1) * bk // DIAG_SPLIT, True)
        else:
            piece(0, bq, bk, True)

    @pl.when(ki == (qi + 1) * kper - 1)
    def _():
        for h in range(N_HEADS):
            acc = acc_sc[h]
            ot_sc[h * V_DIM:(h + 1) * V_DIM, :] = acc[:V_DIM, :] / acc[V_DIM:V_DIM + 1, :]
        o_ref[0] = ot_sc[...].T.astype(BF16)


def _attn_call(q, k, vt, bq, bk):
    bsz, seq, _ = q.shape
    kper = bq // bk
    pairs = [(i, j) for i in range(seq // bq) for j in range((i + 1) * kper)]
    qtile = jnp.asarray(np.array([p[0] for p in pairs], np.int32))
    ktile = jnp.asarray(np.array([p[1] for p in pairs], np.int32))
    grid_spec = pltpu.PrefetchScalarGridSpec(
        num_scalar_prefetch=2,
        grid=(bsz, len(pairs)),
        in_specs=[pl.BlockSpec((1, bq, QK_PAD), lambda b, t, qt, kt: (b, qt[t], 0)),
                  pl.BlockSpec((1, bk, QK_PAD), lambda b, t, qt, kt: (b, kt[t], 0)),
                  pl.BlockSpec((1, QK_PAD, bk), lambda b, t, qt, kt: (b, 0, kt[t]))],
        out_specs=pl.BlockSpec((1, bq, ATTN_WIDTH), lambda b, t, qt, kt: (b, qt[t], 0)),
        scratch_shapes=[pltpu.VMEM((N_HEADS, 1, bq), F32), pltpu.VMEM((N_HEADS, HEAD_PAD, bq), F32),
                        pltpu.VMEM((ATTN_WIDTH, bq), F32)],
    )
    return pl.pallas_call(
        functools.partial(_attn_kernel, bq=bq, bk=bk),
        grid_spec=grid_spec,
        out_shape=jax.ShapeDtypeStruct((bsz, seq, ATTN_WIDTH), BF16),
        compiler_params=pltpu.CompilerParams(dimension_semantics=("arbitrary", "arbitrary"),
                                             vmem_limit_bytes=VMEM_LIMIT),
        name="prompt_attention",
    )(qtile, ktile, q, k, vt)


def _post_kernel(o_ref, ga_ref, gp_ref, sga_ref, x_ref, p_ref, lnig_ref, lnib_ref, attnup_ref, wout_ref, pgate_ref,
                 pproj_ref, lng_ref, lnb_ref, y_ref, *, input_ln):
    og = (o_ref[...].astype(F32) * ga_ref[...].astype(F32)).astype(BF16)
    b_attn = _dot(og, attnup_ref[...])
    mix = gp_ref[...].astype(F32) + sga_ref[...].astype(F32) * b_attn
    sub = _dot(mix.astype(BF16), wout_ref[...])
    x = x_ref[...]
    if input_ln:
        x = _layer_norm(x, lnig_ref[...], lnib_ref[...])
    ple = jax.nn.sigmoid(_dot(x.astype(BF16), pgate_ref[...])) * _dot(p_ref[0].astype(BF16), pproj_ref[...])
    y_ref[...] = _layer_norm(DN_ALPHA * x + sub + ple, lng_ref[...], lnb_ref[...])


def _post_call(o, ga, gp, sga, x, p_all, layer, ln_in, wl, tm, input_ln):
    n = x.shape[0]
    row = lambda width: pl.BlockSpec((tm, width), lambda i: (i, 0))
    return pl.pallas_call(
        functools.partial(_post_kernel, input_ln=input_ln),
        grid=(n // tm,),
        in_specs=[row(ATTN_WIDTH), row(ATTN_WIDTH), row(D_MODEL), row(D_MODEL), row(D_MODEL),
                  pl.BlockSpec((1, tm, PLE_DIM), lambda i: (layer, i, 0)),
                  _const_spec((1, D_MODEL)), _const_spec((1, D_MODEL)),
                  _const_spec((ATTN_WIDTH, D_MODEL)), _const_spec((D_MODEL, D_MODEL)),
                  _const_spec((D_MODEL, D_MODEL)), _const_spec((PLE_DIM, D_MODEL)),
                  _const_spec((1, D_MODEL)), _const_spec((1, D_MODEL))],
        out_specs=row(D_MODEL),
        out_shape=jax.ShapeDtypeStruct((n, D_MODEL), F32),
        compiler_params=pltpu.CompilerParams(dimension_semantics=("arbitrary",), vmem_limit_bytes=VMEM_LIMIT),
        name="layer_post",
    )(o, ga, gp, sga, x, p_all, *ln_in, wl["attn_up"], wl["w_out"], wl["ple_gate"], wl["ple_proj"], wl["ln_g"],
      wl["ln_b"])


def _pre_sample_kernel(x_ref, lnig_ref, lnib_ref, state_ref, c_ref, sa_ref, sb_ref, w1t_ref, wkr_ref, poolw_ref,
                       pscale_ref, poolup_ref, qg_ref, wuq_ref, kvg_ref, wukt_ref,
                       q_ref, qlat_ref, lat_ref, krope_ref, gp_ref, sga_ref, ga_ref, pv_ref, *, input_ln):
    x = x_ref[...]
    if input_ln:
        x = _layer_norm(x, lnig_ref[...], lnib_ref[...])
    xb = x.astype(BF16)

    def seg(a, b):
        return _dot_nt(xb, w1t_ref[a:b, :])

    pv = seg(O_PV, O_PG)
    pv_ref[...] = pv
    means = []
    for g, w in enumerate(POOL_WINDOWS):
        c0 = g * POOL_GC
        acc = pv[:, c0:c0 + POOL_GC]
        for j in range(1, w):
            acc = acc + state_ref[POOL_BUF - j, :, c0:c0 + POOL_GC]
        means.append(acc / float(w) - pv[:, c0:c0 + POOL_GC])
    gp_ref[...] = _pool_tail(means, seg(O_PG, O_CQ), seg(O_MGP, O_MGA), poolw_ref, pscale_ref,
                             poolup_ref).astype(BF16)
    sga_ref[...] = jax.nn.sigmoid(seg(O_MGA, O_END)).astype(BF16)
    ga_ref[...] = _silu(seg(O_AG, O_MGP)).astype(BF16)

    c, sa, sb = c_ref[...], sa_ref[...], sb_ref[...]
    qh = _q_heads(seg(O_CQ, O_CKV), qg_ref, wuq_ref, c, sa, sb)
    qf = jnp.concatenate(qh, axis=-1)
    q_ref[...] = qf
    for h in range(N_HEADS):
        qlat_ref[:, h * KV_LORA:(h + 1) * KV_LORA] = _dot(
            qh[h], wukt_ref[h * HEAD_PAD:(h + 1) * HEAD_PAD, :]).astype(BF16)
    lat_ref[...] = _rms_norm(seg(O_CKV, O_KR), kvg_ref[...])
    krope_ref[...] = _rope_key(_dot_nt(xb, wkr_ref[...]), c, sa, sb)[:, HEAD_PAD - QK_ROPE:]


def _pre_sample_call(x, ln_in, state_t, tabs, wl, input_ln):
    n = x.shape[0]
    sds = jax.ShapeDtypeStruct
    out_shape = [sds((n, QK_PAD), BF16), sds((n, N_HEADS * KV_LORA), BF16), sds((n, KV_LORA), F32),
                 sds((n, QK_ROPE), F32), sds((n, D_MODEL), BF16), sds((n, D_MODEL), BF16),
                 sds((n, ATTN_WIDTH), BF16), sds((n, POOL_WIDTH), F32)]
    return pl.pallas_call(
        functools.partial(_pre_sample_kernel, input_ln=input_ln),
        out_shape=out_shape,
        compiler_params=pltpu.CompilerParams(vmem_limit_bytes=VMEM_LIMIT),
        name="sample_pre",
    )(x, *ln_in, state_t, *tabs, wl["w1t"], wl["wkr"], wl["pool_w"], wl["pool_scale"], wl["pool_up"], wl["q_norm_g"], wl["wuq"],
      wl["kv_norm_g"], wl["wukt"])


def _decode_kernel(pt_ref, qlat_ref, qrope_ref, latn_ref, ropen_ref, clat_hbm, cropet_hbm, o_ref,
                   latbuf, ropebuf, latb16, p_sc, tail_sc, den_sc, sem, *, layer, n_pages):
    g = pl.program_id(0)
    nb = pl.num_programs(0) - 1
    slot = g % 2
    prev = 1 - slot
    past = n_pages * PAGE_SIZE

    def page_copies(sl, p, page):
        rows = pl.ds(p * PAGE_SIZE, PAGE_SIZE)
        return (pltpu.make_async_copy(clat_hbm.at[layer, page], latbuf.at[sl, rows], sem.at[0, sl]),
                pltpu.make_async_copy(cropet_hbm.at[layer, page], ropebuf.at[sl, p], sem.at[1, sl]))

    def start_all(bb, sl):
        def body(p, carry):
            for cp in page_copies(sl, p, pt_ref[bb, p]):
                cp.start()
            return carry
        lax.fori_loop(0, n_pages, body, 0)

    @pl.when(g == 0)
    def _():
        start_all(0, 0)
        latb16[1] = jnp.zeros((past, KV_LORA), BF16)
        p_sc[1] = jnp.zeros((N_HEADS, past), BF16)
        tail_sc[1] = jnp.zeros((N_HEADS, KV_LORA), F32)
        den_sc[1] = jnp.ones((N_HEADS, 1), F32)

    @pl.when(g + 1 < nb)
    def _():
        start_all(g + 1, prev)

    @pl.when(g < nb)
    def _():
        for p in range(n_pages):
            for cp in page_copies(slot, p, 0):
                cp.wait()

    ql = qlat_ref[0]
    qr = qrope_ref[0]
    latn = latn_ref[0]
    ropen = ropen_ref[0]
    latb16[slot] = latbuf[slot].astype(BF16)
    s_rope = jnp.concatenate([_dot(qr, ropebuf[slot, p].astype(BF16)) for p in range(n_pages)], axis=1)
    s = _dot_nt(ql, latb16[slot]) + s_rope
    s_new = (jnp.sum(ql.astype(F32) * latn, -1, keepdims=True)
             + jnp.sum(qr.astype(F32) * ropen, -1, keepdims=True))
    m = jnp.maximum(jnp.max(s, -1, keepdims=True), s_new)
    p = jnp.exp2(s - m)
    p_new = jnp.exp2(s_new - m)

    o_ref[0] = (_dot(p_sc[prev], latb16[prev]) + tail_sc[prev]) / den_sc[prev]

    p_sc[slot] = p.astype(BF16)
    tail_sc[slot] = p_new * latn
    den_sc[slot] = jnp.sum(p, -1, keepdims=True) + p_new


def _decode_call(page_table, qlat, qrope, lat_new, rope_new, cache_lat, cache_rope, layer):
    bsz, n_pages = page_table.shape
    past = n_pages * PAGE_SIZE
    cur = lambda s1, s2: pl.BlockSpec((1, s1, s2), lambda g, pt: (jnp.minimum(g, bsz - 1), 0, 0))
    grid_spec = pltpu.PrefetchScalarGridSpec(
        num_scalar_prefetch=1,
        grid=(bsz + 1,),
        in_specs=[cur(N_HEADS, KV_LORA), cur(N_HEADS, QK_ROPE), cur(1, KV_LORA), cur(1, QK_ROPE),
                  pl.BlockSpec(memory_space=pl.ANY), pl.BlockSpec(memory_space=pl.ANY)],
        out_specs=pl.BlockSpec((1, N_HEADS, KV_LORA), lambda g, pt: (jnp.maximum(g - 1, 0), 0, 0)),
        scratch_shapes=[pltpu.VMEM((2, past, KV_LORA), F32), pltpu.VMEM((2, n_pages, QK_ROPE, PAGE_SIZE), F32),
                        pltpu.VMEM((2, past, KV_LORA), BF16), pltpu.VMEM((2, N_HEADS, past), BF16),
                        pltpu.VMEM((2, N_HEADS, KV_LORA), F32), pltpu.VMEM((2, N_HEADS, 1), F32),
                        pltpu.SemaphoreType.DMA((2, 2))],
    )
    return pl.pallas_call(
        functools.partial(_decode_kernel, layer=layer, n_pages=n_pages),
        grid_spec=grid_spec,
        out_shape=jax.ShapeDtypeStruct((bsz, N_HEADS, KV_LORA), F32),
        compiler_params=pltpu.CompilerParams(dimension_semantics=("arbitrary",), vmem_limit_bytes=VMEM_LIMIT),
        name="sample_decode_attention",
    )(page_table, qlat, qrope, lat_new, rope_new, cache_lat, cache_rope)


def _olat_kernel(olat_ref, wuvt_ref, o_ref):
    outs = [_dot_nt(olat_ref[:, h * KV_LORA:(h + 1) * KV_LORA].astype(BF16),
                    wuvt_ref[h * HEAD_PAD:(h + 1) * HEAD_PAD, :]) for h in range(N_HEADS)]
    for hp in range(N_HEADS // 2):
        pair = outs[2 * hp] + pltpu.roll(outs[2 * hp + 1], V_DIM, 1)
        o_ref[:, hp * HEAD_PAD:(hp + 1) * HEAD_PAD] = pair.astype(BF16)


def _olat_call(olat_flat, wuvt):
    n = olat_flat.shape[0]
    return pl.pallas_call(
        _olat_kernel,
        out_shape=jax.ShapeDtypeStruct((n, ATTN_WIDTH), BF16),
        name="sample_value_up",
    )(olat_flat, wuvt)


def _rope_tables(pos):
    lane = jnp.arange(HEAD_PAD)
    freqs = ROPE_THETA ** (-(lane % ROPE_HALF).astype(F32) / ROPE_HALF)
    ang = pos.astype(F32)[:, None] * freqs[None, :]
    cos, sin = jnp.cos(ang), jnp.sin(ang)
    is_rope = lane >= QK_NOPE
    first_half = is_rope & ((lane // ROPE_HALF) % 2 == 0)
    second_half = is_rope & ((lane // ROPE_HALF) % 2 == 1)
    c = jnp.where(is_rope[None, :], cos, 1.0)
    sa = jnp.where(first_half[None, :], -sin, 0.0)
    sb = jnp.where(second_half[None, :], sin, 0.0)
    return c, sa, sb


def _prep_layer(i, w_in, pool_w, pool_scale, pool_up, q_norm_g, w_uq, kv_norm_g, w_uk, w_uv, attn_up, w_out,
                ple_proj, ple_gate, ln_g, ln_b):
    w1t = jnp.swapaxes(w_in[i], 0, 1)
    wkr_t = w1t[O_KR:O_AG]
    wkr = jnp.concatenate([jnp.zeros((QK_NOPE, D_MODEL), F32), wkr_t, wkr_t], axis=0)
    pad_head = lambda a: jnp.pad(a, ((0, 0), (0, 0), (0, HEAD_PAD - a.shape[-1]))).reshape(a.shape[0], QK_PAD)
    wukt =jnp.pad(jnp.transpose(w_uk[i], (1, 2, 0)), ((0, 0), (0, HEAD_PAD - QK_NOPE), (0, 0)))
    wuvt = jnp.pad(jnp.transpose(w_uv[i], (1, 2, 0)), ((0, 0), (0, HEAD_PAD - V_DIM), (0, 0)))
    return {
        "w1t": w1t.astype(BF16),
        "wkr": wkr.astype(BF16),
        "pool_w": pool_w[i].astype(BF16),
        "pool_scale": pool_scale[i].reshape(1, POOL_WIDTH),
        "pool_up": pool_up[i].astype(BF16),
        "q_norm_g": q_norm_g[i].reshape(1, Q_LORA),
        "wuq": pad_head(w_uq[i].reshape(Q_LORA, N_HEADS, QK_HEAD)).astype(BF16),
        "kv_norm_g": kv_norm_g[i].reshape(1, KV_LORA),
        "wukt": wukt.reshape(QK_PAD, KV_LORA).astype(BF16),
        "wuvt": wuvt.reshape(QK_PAD, KV_LORA).astype(BF16),
        "attn_up": attn_up[i].astype(BF16),
        "w_out": w_out[i].astype(BF16),
        "ple_gate": ple_gate[i].astype(BF16),
        "ple_proj": ple_proj[i].astype(BF16),
        "ln_g": ln_g[i].reshape(1, D_MODEL),
        "ln_b": ln_b[i].reshape(1, D_MODEL),
    }


def kernel(x_prompt, x_sample, cache_kv_latent, cache_k_rope, state_pool, page_table, p_prompt, p_sample,
           ln_in_g, ln_in_b, w_in, pool_w, pool_scale, pool_up, q_norm_g, w_uq, kv_norm_g, w_uk, w_uv,
           attn_up, w_out, ple_proj, ple_gate, ln_g, ln_b):
    bsz, seq, _ = x_prompt.shape
    nsmp = x_sample.shape[0]
    n_pages = page_table.shape[1]
    past_len = n_pages * PAGE_SIZE
    n_tok = bsz * seq
    tm_pre, tm_post, bq, bk = 512, 512, 1024, 1024

    tabs_p = _rope_tables(jnp.arange(seq))
    tabs_s = _rope_tables(jnp.full((1,), past_len))
    ln_in = (ln_in_g.reshape(1, D_MODEL), ln_in_b.reshape(1, D_MODEL))
    cache_rope_t = jnp.swapaxes(cache_k_rope, 2, 3)

    hp = x_prompt.reshape(n_tok, D_MODEL)
    hs = x_sample.reshape(nsmp, D_MODEL)

    lat_p, rope_p, pool_p, lat_s, rope_s, pool_s = [], [], [], [], [], []
    for i in range(DEPTH):
        first = i == 0
        wl = _prep_layer(i, w_in, pool_w, pool_scale, pool_up, q_norm_g, w_uq, kv_norm_g, w_uk, w_uv, attn_up,
                         w_out, ple_proj, ple_gate, ln_g, ln_b)
        q, k, vt, lat, krope, gp, sga, ga, pstate = _pre_prompt_call(hp.reshape(bsz, seq, D_MODEL), ln_in, tabs_p,
                                                                     wl, tm_pre, first)
        o = _attn_call(q, k, vt, bq, bk)
        hp = _post_call(o.reshape(n_tok, ATTN_WIDTH), ga.reshape(n_tok, ATTN_WIDTH), gp.reshape(n_tok, D_MODEL),
                        sga.reshape(n_tok, D_MODEL), hp, p_prompt.reshape(DEPTH, n_tok, PLE_DIM), i, ln_in, wl, tm_post,
                        first)
        lat_p.append(lat)
        rope_p.append(krope)
        pool_p.append(pstate[:, 1:, :])
        qs, qlat, lat_n, rope_n, gp_s, sga_s, ga_s, pv_s = _pre_sample_call(
            hs, ln_in, jnp.swapaxes(state_pool[i], 0, 1), tabs_s, wl, first)
        qrope = qs.reshape(nsmp, N_HEADS, HEAD_PAD)[:, :, QK_NOPE:QK_HEAD]
        olat = _decode_call(page_table, qlat.reshape(nsmp, N_HEADS, KV_LORA), qrope,
                            lat_n.reshape(nsmp, 1, KV_LORA), rope_n.reshape(nsmp, 1, QK_ROPE),
                            cache_kv_latent, cache_rope_t, i)
        o_s = _olat_call(olat.reshape(nsmp, N_HEADS * KV_LORA), wl["wuvt"])
        hs = _post_call(o_s, ga_s, gp_s, sga_s, hs, p_sample.reshape(DEPTH, nsmp, PLE_DIM), i, ln_in, wl, nsmp,
                        first)
        lat_s.append(lat_n.reshape(nsmp, 1, KV_LORA))
        rope_s.append(rope_n.reshape(nsmp, 1, QK_ROPE))
        pool_s.append(jnp.concatenate([state_pool[i][:, 1:, :], pv_s[:, None, :]], axis=1))

    return (hp.reshape(bsz, seq, D_MODEL), hs.reshape(nsmp, 1, D_MODEL), jnp.stack(lat_p), jnp.stack(rope_p),
            jnp.stack(pool_p), jnp.stack(lat_s), jnp.stack(rope_s), jnp.stack(pool_s))
```

```python
import functools
import math

import jax
import jax.numpy as jnp
import numpy as np
from jax import lax
from jax.experimental import pallas as pl
from jax.experimental.pallas import tpu as pltpu

F32 = jnp.float32
BF16 = jnp.bfloat16

D_MODEL = 1024
DEPTH = 2
PAGE_SIZE = 128
POOL_WINDOWS = (2, 4, 8, 16)
POOL_WIDTH = 512
POOL_GC = 128
POOL_WMAX = 16
POOL_BUF = POOL_WMAX - 1
N_HEADS = 8
QK_NOPE = 64
QK_ROPE = 32
ROPE_HALF = QK_ROPE // 2
QK_HEAD = QK_NOPE + QK_ROPE
V_DIM = 64
Q_LORA = 384
KV_LORA = 256
ATTN_WIDTH = N_HEADS * V_DIM
ROPE_THETA = 10000.0
PLE_DIM = 256
DN_ALPHA = (2 * DEPTH) ** 0.25
NORM_EPS = 1e-6
HEAD_PAD = 128
QK_PAD = N_HEADS * HEAD_PAD
Q_SCALE = (QK_HEAD ** -0.5) * math.log2(math.e)

O_PV, O_PG, O_CQ, O_CKV, O_KR, O_AG, O_MGP, O_MGA, O_END = 0, 512, 1024, 1408, 1664, 1696, 2208, 3232, 4256

VMEM_LIMIT = 56 * 1024 * 1024


def _const_spec(shape):
    nd = len(shape)
    return pl.BlockSpec(shape, lambda *_: (0,) * nd, pipeline_mode=pl.Buffered(1))


def _layer_norm(r, g, b):
    mu = jnp.mean(r, -1, keepdims=True)
    d = r - mu
    var = jnp.mean(d * d, -1, keepdims=True)
    return d * lax.rsqrt(var + NORM_EPS) * g + b


def _rms_norm(x, g):
    return x * lax.rsqrt(jnp.mean(x * x, -1, keepdims=True) + NORM_EPS) * g


def _silu(x):
    return x * jax.nn.sigmoid(x)


def _dot(a, b):
    return jnp.dot(a, b, preferred_element_type=F32)


def _dot_nt(a, b):
    return lax.dot_general(a, b, (((1,), (1,)), ((), ())), preferred_element_type=F32)


def _rope_block(blk, c, sa, sb):
    return blk * c + pltpu.roll(blk, HEAD_PAD - ROPE_HALF, 1) * sa + pltpu.roll(blk, ROPE_HALF, 1) * sb


def _pool_tail(pool_mean_minus_tok, pg, mgp, poolw_ref, pscale_ref, poolup_ref):
    ys = [_dot(m.astype(BF16), poolw_ref[g]) for g, m in enumerate(pool_mean_minus_tok)]
    y = jnp.concatenate(ys, axis=-1)
    yb = (y * pscale_ref[...] * _silu(pg)).astype(BF16)
    return jax.nn.sigmoid(mgp) * _dot(yb, poolup_ref[...])


def _q_heads(cq, qg_ref, wuq_ref, c, sa, sb):
    qf = _dot(_rms_norm(cq, qg_ref[...]).astype(BF16), wuq_ref[...])
    return [(_rope_block(qf[:, h * HEAD_PAD:(h + 1) * HEAD_PAD], c, sa, sb) * Q_SCALE).astype(BF16)
            for h in range(N_HEADS)]


def _rope_key(kr, c, sa, sb):
    return _rope_block(kr, c, sa, sb)


def _pre_prompt_kernel(x_ref, lnig_ref, lnib_ref, c_ref, sa_ref, sb_ref, w1t_ref, wkr_ref, poolw_ref, pscale_ref,
                       poolup_ref, qg_ref, wuq_ref, kvg_ref, wukt_ref, wuvt_ref,
                       q_ref, k_ref, vt_ref, lat_ref, krope_ref, gp_ref, sga_ref, ga_ref, pstate_ref,
                       ext_sc, *, tm, input_ln):
    si = pl.program_id(1)
    x = x_ref[0]
    if input_ln:
        x = _layer_norm(x, lnig_ref[...], lnib_ref[...])
    xb = x.astype(BF16)

    def seg(a, b):
        return _dot_nt(xb, w1t_ref[a:b, :])

    pv = seg(O_PV, O_PG)

    @pl.when(si == 0)
    def _():
        ext_sc[0:POOL_WMAX, :] = jnp.zeros((POOL_WMAX, POOL_WIDTH), F32)

    ext_sc[POOL_WMAX:POOL_WMAX + tm, :] = pv
    pos = si * tm + lax.broadcasted_iota(jnp.int32, (tm, 1), 0)
    means = []
    for g, w in enumerate(POOL_WINDOWS):
        c0 = g * POOL_GC
        acc = pv[:, c0:c0 + POOL_GC]
        for j in range(1, w):
            acc = acc + ext_sc[POOL_WMAX - j:POOL_WMAX - j + tm, c0:c0 + POOL_GC]
        inv_cnt = 1.0 / jnp.minimum(pos + 1, w).astype(F32)
        means.append(acc * inv_cnt - pv[:, c0:c0 + POOL_GC])
    tail = ext_sc[tm:tm + POOL_WMAX, :]
    pstate_ref[0] = tail
    ext_sc[0:POOL_WMAX, :] = tail
    gp_ref[0] = _pool_tail(means, seg(O_PG, O_CQ), seg(O_MGP, O_MGA), poolw_ref, pscale_ref,
                           poolup_ref).astype(BF16)
    sga_ref[0] = jax.nn.sigmoid(seg(O_MGA, O_END)).astype(BF16)
    ga_ref[0] = _silu(seg(O_AG, O_MGP)).astype(BF16)

    c, sa, sb = c_ref[...], sa_ref[...], sb_ref[...]
    for h, qh in enumerate(_q_heads(seg(O_CQ, O_CKV), qg_ref, wuq_ref, c, sa, sb)):
        q_ref[0, :, h * HEAD_PAD:(h + 1) * HEAD_PAD] = qh

    lat = _rms_norm(seg(O_CKV, O_KR), kvg_ref[...])
    lat_ref[0] = lat
    latb = lat.astype(BF16)
    krr = _rope_key(_dot_nt(xb, wkr_ref[...]), c, sa, sb)
    krope_ref[0] = krr[:, HEAD_PAD - QK_ROPE:]
    lane = lax.broadcasted_iota(jnp.int32, (tm, HEAD_PAD), 1)
    krk = jnp.where(lane < QK_HEAD, krr, 0.0)
    kn = _dot_nt(latb, wukt_ref[...])
    for h in range(N_HEADS):
        k_ref[0, :, h * HEAD_PAD:(h + 1) * HEAD_PAD] = (kn[:, h * HEAD_PAD:(h + 1) * HEAD_PAD] + krk).astype(BF16)
    vt = _dot_nt(wuvt_ref[...], latb)
    rowi = lax.broadcasted_iota(jnp.int32, (QK_PAD, tm), 0)
    vt_ref[0] = jnp.where((rowi & (HEAD_PAD - 1)) >= V_DIM, 1.0, vt).astype(BF16)


def _pre_prompt_call(x, ln_in, tabs, wl, tm, input_ln):
    bsz, seq, _ = x.shape
    row = lambda width: pl.BlockSpec((1, tm, width), lambda b, s: (b, s, 0))
    tab = pl.BlockSpec((tm, HEAD_PAD), lambda b, s: (s, 0))
    in_specs = [row(D_MODEL), _const_spec((1, D_MODEL)), _const_spec((1, D_MODEL)), tab, tab, tab,
                _const_spec((O_END, D_MODEL)), _const_spec((HEAD_PAD, D_MODEL)), _const_spec((4, POOL_GC, POOL_GC)),
                _const_spec((1, POOL_WIDTH)), _const_spec((POOL_WIDTH, D_MODEL)), _const_spec((1, Q_LORA)),
                _const_spec((Q_LORA, QK_PAD)), _const_spec((1, KV_LORA)), _const_spec((QK_PAD, KV_LORA)),
                _const_spec((QK_PAD, KV_LORA))]
    out_specs = [row(QK_PAD), row(QK_PAD), pl.BlockSpec((1, QK_PAD, tm), lambda b, s: (b, 0, s)),
                 row(KV_LORA), row(QK_ROPE), row(D_MODEL), row(D_MODEL),
                 row(ATTN_WIDTH), pl.BlockSpec((1, POOL_WMAX, POOL_WIDTH), lambda b, s: (b, 0, 0))]
    sds = jax.ShapeDtypeStruct
    out_shape = [sds((bsz, seq, QK_PAD), BF16), sds((bsz, seq, QK_PAD), BF16), sds((bsz, QK_PAD, seq), BF16),
                 sds((bsz, seq, KV_LORA), F32), sds((bsz, seq, QK_ROPE), F32), sds((bsz, seq, D_MODEL), BF16),
                 sds((bsz, seq, D_MODEL), BF16), sds((bsz, seq, ATTN_WIDTH), BF16),
                 sds((bsz, POOL_WMAX, POOL_WIDTH), F32)]
    return pl.pallas_call(
        functools.partial(_pre_prompt_kernel, tm=tm, input_ln=input_ln),
        grid=(bsz, seq // tm),
        in_specs=in_specs, out_specs=out_specs, out_shape=out_shape,
        scratch_shapes=[pltpu.VMEM((tm + POOL_WMAX, POOL_WIDTH), F32)],
        compiler_params=pltpu.CompilerParams(dimension_semantics=("arbitrary", "arbitrary"),
                                             vmem_limit_bytes=VMEM_LIMIT),
        name="prompt_pre",
    )(x, *ln_in, *tabs, wl["w1t"], wl["wkr"], wl["pool_w"], wl["pool_scale"], wl["pool_up"], wl["q_norm_g"], wl["wuq"],
      wl["kv_norm_g"], wl["wukt"], wl["wuvt"])


DIAG_SPLIT = 4


def _attn_kernel(qtile_ref, ktile_ref, q_ref, k_ref, vt_ref, o_ref, m_sc, acc_sc, ot_sc, *, bq, bk):
    t = pl.program_id(1)
    qi = qtile_ref[t]
    ki = ktile_ref[t]
    kper = bq // bk

    @pl.when(ki == 0)
    def _():
        m_sc[...] = jnp.full(m_sc.shape, -jnp.inf, F32)
        acc_sc[...] = jnp.zeros(acc_sc.shape, F32)

    def piece(qlo, qhi, khi, masked):
        qs = slice(qlo, qhi)
        if masked:
            key_pos = ki * bk + lax.broadcasted_iota(jnp.int32, (khi, qhi - qlo), 0)
            qry_pos = qi * bq + qlo + lax.broadcasted_iota(jnp.int32, (khi, qhi - qlo), 1)
            bias = jnp.where(key_pos <= qry_pos, 0.0, -jnp.inf).astype(BF16)

        def scores(h):
            sl = slice(h * HEAD_PAD, (h + 1) * HEAD_PAD)
            return _dot_nt(k_ref[0, :khi, sl], q_ref[0, qs, sl]).astype(BF16)

        st_next = scores(0)
        for h in range(N_HEADS):
            sl = slice(h * HEAD_PAD, (h + 1) * HEAD_PAD)
            st = st_next
            if h + 1 < N_HEADS:
                st_next = scores(h + 1)
            if masked:
                st = st + bias
            m_prev = m_sc[h, :, qs]
            m_new = jnp.maximum(m_prev, jnp.max(st, axis=0, keepdims=True).astype(F32))
            alpha = jnp.exp2(m_prev - m_new)
            pt = jnp.exp2(st - m_new.astype(BF16))
            acc_sc[h, :, qs] = alpha * acc_sc[h, :, qs] + _dot(vt_ref[0, sl, :khi], pt)
            m_sc[h, :, qs] = m_new

    @pl.when(ki < qi * kper)
    def _():
        piece(0, bq, bk, False)

    @pl.when(jnp.logical_and(ki >= qi * kper, ki < (qi + 1) * kper))
    def _():
        if bq == bk:
            for i in range(DIAG_SPLIT):
                piece(i * bq // DIAG_SPLIT, (i + 1) * bq // DIAG_SPLIT, (i + 1) * bk // DIAG_SPLIT, True)
        else:
            piece(0, bq, bk, True)

    @pl.when(ki == (qi + 1) * kper - 1)
    def _():
        for h in range(N_HEADS):
            acc = acc_sc[h]
            ot_sc[h * V_DIM:(h + 1) * V_DIM, :] = acc[:V_DIM, :] / acc[V_DIM:V_DIM + 1, :]
        o_ref[0] = ot_sc[...].T.astype(BF16)


def _attn_call(q, k, vt, bq, bk):
    bsz, seq, _ = q.shape
    kper = bq // bk
    pairs = [(i, j) for i in range(seq // bq) for j in range((i + 1) * kper)]
    qtile = jnp.asarray(np.array([p[0] for p in pairs], np.int32))
    ktile = jnp.asarray(np.array([p[1] for p in pairs], np.int32))
    grid_spec = pltpu.PrefetchScalarGridSpec(
        num_scalar_prefetch=2,
        grid=(bsz, len(pairs)),
        in_specs=[pl.BlockSpec((1, bq, QK_PAD), lambda b, t, qt, kt: (b, qt[t], 0)),
                  pl.BlockSpec((1, bk, QK_PAD), lambda b, t, qt, kt: (b, kt[t], 0)),
                  pl.BlockSpec((1, QK_PAD, bk), lambda b, t, qt, kt: (b, 0, kt[t]))],
        out_specs=pl.BlockSpec((1, bq, ATTN_WIDTH), lambda b, t, qt, kt: (b, qt[t], 0)),
        scratch_shapes=[pltpu.VMEM((N_HEADS, 1, bq), F32), pltpu.VMEM((N_HEADS, HEAD_PAD, bq), F32),
                        pltpu.VMEM((ATTN_WIDTH, bq), F32)],
    )
    return pl.pallas_call(
        functools.partial(_attn_kernel, bq=bq, bk=bk),
        grid_spec=grid_spec,
        out_shape=jax.ShapeDtypeStruct((bsz, seq, ATTN_WIDTH), BF16),
        compiler_params=pltpu.CompilerParams(dimension_semantics=("arbitrary", "arbitrary"),
                                             vmem_limit_bytes=VMEM_LIMIT),
        name="prompt_attention",
    )(qtile, ktile, q, k, vt)


def _post_kernel(o_ref, ga_ref, gp_ref, sga_ref, x_ref, p_ref, lnig_ref, lnib_ref, attnup_ref, wout_ref, pgate_ref,
                 pproj_ref, lng_ref, lnb_ref, y_ref, *, input_ln):
    og = (o_ref[...].astype(F32) * ga_ref[...].astype(F32)).astype(BF16)
    b_attn = _dot(og, attnup_ref[...])
    mix = gp_ref[...].astype(F32) + sga_ref[...].astype(F32) * b_attn
    sub = _dot(mix.astype(BF16), wout_ref[...])
    x = x_ref[...]
    if input_ln:
        x = _layer_norm(x, lnig_ref[...], lnib_ref[...])
    ple = jax.nn.sigmoid(_dot(x.astype(BF16), pgate_ref[...])) * _dot(p_ref[0].astype(BF16), pproj_ref[...])
    y_ref[...] = _layer_norm(DN_ALPHA * x + sub + ple, lng_ref[...], lnb_ref[...])


def _post_call(o, ga, gp, sga, x, p_all, layer, ln_in, wl, tm, input_ln):
    n = x.shape[0]
    row = lambda width: pl.BlockSpec((tm, width), lambda i: (i, 0))
    return pl.pallas_call(
        functools.partial(_post_kernel, input_ln=input_ln),
        grid=(n // tm,),
        in_specs=[row(ATTN_WIDTH), row(ATTN_WIDTH), row(D_MODEL), row(D_MODEL), row(D_MODEL),
                  pl.BlockSpec((1, tm, PLE_DIM), lambda i: (layer, i, 0)),
                  _const_spec((1, D_MODEL)), _const_spec((1, D_MODEL)),
                  _const_spec((ATTN_WIDTH, D_MODEL)), _const_spec((D_MODEL, D_MODEL)),
                  _const_spec((D_MODEL, D_MODEL)), _const_spec((PLE_DIM, D_MODEL)),
                  _const_spec((1, D_MODEL)), _const_spec((1, D_MODEL))],
        out_specs=row(D_MODEL),
        out_shape=jax.ShapeDtypeStruct((n, D_MODEL), F32),
        compiler_params=pltpu.CompilerParams(dimension_semantics=("arbitrary",), vmem_limit_bytes=VMEM_LIMIT),
        name="layer_post",
    )(o, ga, gp, sga, x, p_all, *ln_in, wl["attn_up"], wl["w_out"], wl["ple_gate"], wl["ple_proj"], wl["ln_g"],
      wl["ln_b"])


def _pre_sample_kernel(x_ref, lnig_ref, lnib_ref, state_ref, c_ref, sa_ref, sb_ref, w1t_ref, wkr_ref, poolw_ref,
                       pscale_ref, poolup_ref, qg_ref, wuq_ref, kvg_ref, wukt_ref,
                       q_ref, qlat_ref, lat_ref, krope_ref, gp_ref, sga_ref, ga_ref, pv_ref, *, input_ln):
    x = x_ref[...]
    if input_ln:
        x = _layer_norm(x, lnig_ref[...], lnib_ref[...])
    xb = x.astype(BF16)

    def seg(a, b):
        return _dot_nt(xb, w1t_ref[a:b, :])

    pv = seg(O_PV, O_PG)
    pv_ref[...] = pv
    means = []
    for g, w in enumerate(POOL_WINDOWS):
        c0 = g * POOL_GC
        acc = pv[:, c0:c0 + POOL_GC]
        for j in range(1, w):
            acc = acc + state_ref[POOL_BUF - j, :, c0:c0 + POOL_GC]
        means.append(acc / float(w) - pv[:, c0:c0 + POOL_GC])
    gp_ref[...] = _pool_tail(means, seg(O_PG, O_CQ), seg(O_MGP, O_MGA), poolw_ref, pscale_ref,
                             poolup_ref).astype(BF16)
    sga_ref[...] = jax.nn.sigmoid(seg(O_MGA, O_END)).astype(BF16)
    ga_ref[...] = _silu(seg(O_AG, O_MGP)).astype(BF16)

    c, sa, sb = c_ref[...], sa_ref[...], sb_ref[...]
    qh = _q_heads(seg(O_CQ, O_CKV), qg_ref, wuq_ref, c, sa, sb)
    qf = jnp.concatenate(qh, axis=-1)
    q_ref[...] = qf
    for h in range(N_HEADS):
        qlat_ref[:, h * KV_LORA:(h + 1) * KV_LORA] = _dot(
            qh[h], wukt_ref[h * HEAD_PAD:(h + 1) * HEAD_PAD, :]).astype(BF16)
    lat_ref[...] = _rms_norm(seg(O_CKV, O_KR), kvg_ref[...])
    krope_ref[...] = _rope_key(_dot_nt(xb, wkr_ref[...]), c, sa, sb)[:, HEAD_PAD - QK_ROPE:]


def _pre_sample_call(x, ln_in, state_t, tabs, wl, input_ln):
    n = x.shape[0]
    sds = jax.ShapeDtypeStruct
    out_shape = [sds((n, QK_PAD), BF16), sds((n, N_HEADS * KV_LORA), BF16), sds((n, KV_LORA), F32),
                 sds((n, QK_ROPE), F32), sds((n, D_MODEL), BF16), sds((n, D_MODEL), BF16),
                 sds((n, ATTN_WIDTH), BF16), sds((n, POOL_WIDTH), F32)]
    return pl.pallas_call(
        functools.partial(_pre_sample_kernel, input_ln=input_ln),
        out_shape=out_shape,
        compiler_params=pltpu.CompilerParams(vmem_limit_bytes=VMEM_LIMIT),
        name="sample_pre",
    )(x, *ln_in, state_t, *tabs, wl["w1t"], wl["wkr"], wl["pool_w"], wl["pool_scale"], wl["pool_up"], wl["q_norm_g"], wl["wuq"],
      wl["kv_norm_g"], wl["wukt"])


def _decode_kernel(pt_ref, qlat_ref, qrope_ref, latn_ref, ropen_ref, clat_hbm, cropet_hbm, o_ref,
                   latbuf, ropebuf, latb16, p_sc, tail_sc, den_sc, sem, *, layer, n_pages):
    g = pl.program_id(0)
    nb = pl.num_programs(0) - 1
    slot = g % 2
    prev = 1 - slot
    past = n_pages * PAGE_SIZE

    def page_copies(sl, p, page):
        rows = pl.ds(p * PAGE_SIZE, PAGE_SIZE)
        return (pltpu.make_async_copy(clat_hbm.at[layer, page], latbuf.at[sl, rows], sem.at[0, sl]),
                pltpu.make_async_copy(cropet_hbm.at[layer, page], ropebuf.at[sl, p], sem.at[1, sl]))

    def start_all(bb, sl):
        def body(p, carry):
            for prio, cp in enumerate(page_copies(sl, p, pt_ref[bb, p])):
                cp.start(priority=prio)
            return carry
        lax.fori_loop(0, n_pages, body, 0)

    @pl.when(g == 0)
    def _():
        start_all(0, 0)
        latb16[1] = jnp.zeros((past, KV_LORA), BF16)
        p_sc[1] = jnp.zeros((N_HEADS, past), BF16)
        tail_sc[1] = jnp.zeros((N_HEADS, KV_LORA), F32)
        den_sc[1] = jnp.ones((N_HEADS, 1), F32)

    @pl.when(g + 1 < nb)
    def _():
        start_all(g + 1, prev)

    @pl.when(g < nb)
    def _():
        for p in range(n_pages):
            for cp in page_copies(slot, p, 0):
                cp.wait()

    ql = qlat_ref[0]
    qr = qrope_ref[0]
    latn = latn_ref[0]
    ropen = ropen_ref[0]
    latb16[slot] = latbuf[slot].astype(BF16)
    s_rope = jnp.concatenate([_dot(qr, ropebuf[slot, p].astype(BF16)) for p in range(n_pages)], axis=1)
    s = _dot_nt(ql, latb16[slot]) + s_rope
    s_new = (jnp.sum(ql.astype(F32) * latn, -1, keepdims=True)
             + jnp.sum(qr.astype(F32) * ropen, -1, keepdims=True))
    m = jnp.maximum(jnp.max(s, -1, keepdims=True), s_new)
    p = jnp.exp2(s - m)
    p_new = jnp.exp2(s_new - m)

    o_ref[0] = (_dot(p_sc[prev], latb16[prev]) + tail_sc[prev]) / den_sc[prev]

    p_sc[slot] = p.astype(BF16)
    tail_sc[slot] = p_new * latn
    den_sc[slot] = jnp.sum(p, -1, keepdims=True) + p_new


def _decode_call(page_table, qlat, qrope, lat_new, rope_new, cache_lat, cache_rope, layer):
    bsz, n_pages = page_table.shape
    past = n_pages * PAGE_SIZE
    cur = lambda s1, s2: pl.BlockSpec((1, s1, s2), lambda g, pt: (jnp.minimum(g, bsz - 1), 0, 0))
    grid_spec = pltpu.PrefetchScalarGridSpec(
        num_scalar_prefetch=1,
        grid=(bsz + 1,),
        in_specs=[cur(N_HEADS, KV_LORA), cur(N_HEADS, QK_ROPE), cur(1, KV_LORA), cur(1, QK_ROPE),
                  pl.BlockSpec(memory_space=pl.ANY), pl.BlockSpec(memory_space=pl.ANY)],
        out_specs=pl.BlockSpec((1, N_HEADS, KV_LORA), lambda g, pt: (jnp.maximum(g - 1, 0), 0, 0)),
        scratch_shapes=[pltpu.VMEM((2, past, KV_LORA), F32), pltpu.VMEM((2, n_pages, QK_ROPE, PAGE_SIZE), F32),
                        pltpu.VMEM((2, past, KV_LORA), BF16), pltpu.VMEM((2, N_HEADS, past), BF16),
                        pltpu.VMEM((2, N_HEADS, KV_LORA), F32), pltpu.VMEM((2, N_HEADS, 1), F32),
                        pltpu.SemaphoreType.DMA((2, 2))],
    )
    return pl.pallas_call(
        functools.partial(_decode_kernel, layer=layer, n_pages=n_pages),
        grid_spec=grid_spec,
        out_shape=jax.ShapeDtypeStruct((bsz, N_HEADS, KV_LORA), F32),
        compiler_params=pltpu.CompilerParams(dimension_semantics=("arbitrary",), vmem_limit_bytes=VMEM_LIMIT),
        name="sample_decode_attention",
    )(page_table, qlat, qrope, lat_new, rope_new, cache_lat, cache_rope)


def _olat_kernel(olat_ref, wuvt_ref, o_ref):
    outs = [_dot_nt(olat_ref[:, h * KV_LORA:(h + 1) * KV_LORA].astype(BF16),
                    wuvt_ref[h * HEAD_PAD:(h + 1) * HEAD_PAD, :]) for h in range(N_HEADS)]
    for hp in range(N_HEADS // 2):
        pair = outs[2 * hp] + pltpu.roll(outs[2 * hp + 1], V_DIM, 1)
        o_ref[:, hp * HEAD_PAD:(hp + 1) * HEAD_PAD] = pair.astype(BF16)


def _olat_call(olat_flat, wuvt):
    n = olat_flat.shape[0]
    return pl.pallas_call(
        _olat_kernel,
        out_shape=jax.ShapeDtypeStruct((n, ATTN_WIDTH), BF16),
        name="sample_value_up",
    )(olat_flat, wuvt)


def _rope_tables(pos):
    lane = jnp.arange(HEAD_PAD)
    freqs = ROPE_THETA ** (-(lane % ROPE_HALF).astype(F32) / ROPE_HALF)
    ang = pos.astype(F32)[:, None] * freqs[None, :]
    cos, sin = jnp.cos(ang), jnp.sin(ang)
    is_rope = lane >= QK_NOPE
    first_half = is_rope & ((lane // ROPE_HALF) % 2 == 0)
    second_half = is_rope & ((lane // ROPE_HALF) % 2 == 1)
    c = jnp.where(is_rope[None, :], cos, 1.0)
    sa = jnp.where(first_half[None, :], -sin, 0.0)
    sb = jnp.where(second_half[None, :], sin, 0.0)
    return c, sa, sb


def _prep_layer(i, w_in, pool_w, pool_scale, pool_up, q_norm_g, w_uq, kv_norm_g, w_uk, w_uv, attn_up, w_out,
                ple_proj, ple_gate, ln_g, ln_b):
    w1t = jnp.swapaxes(w_in[i], 0, 1)
    wkr_t = w1t[O_KR:O_AG]
    wkr = jnp.concatenate([jnp.zeros((QK_NOPE, D_MODEL), F32), wkr_t, wkr_t], axis=0)
    pad_head = lambda a: jnp.pad(a, ((0, 0), (0, 0), (0, HEAD_PAD - a.shape[-1]))).reshape(a.shape[0], QK_PAD)
    wukt =jnp.pad(jnp.transpose(w_uk[i], (1, 2, 0)), ((0, 0), (0, HEAD_PAD - QK_NOPE), (0, 0)))
    wuvt = jnp.pad(jnp.transpose(w_uv[i], (1, 2, 0)), ((0, 0), (0, HEAD_PAD - V_DIM), (0, 0)))
    return {
        "w1t": w1t.astype(BF16),
        "wkr": wkr.astype(BF16),
        "pool_w": pool_w[i].astype(BF16),
        "pool_scale": pool_scale[i].reshape(1, POOL_WIDTH),
        "pool_up": pool_up[i].astype(BF16),
        "q_norm_g": q_norm_g[i].reshape(1, Q_LORA),
        "wuq": pad_head(w_uq[i].reshape(Q_LORA, N_HEADS, QK_HEAD)).astype(BF16),
        "kv_norm_g": kv_norm_g[i].reshape(1, KV_LORA),
        "wukt": wukt.reshape(QK_PAD, KV_LORA).astype(BF16),
        "wuvt": wuvt.reshape(QK_PAD, KV_LORA).astype(BF16),
        "attn_up": attn_up[i].astype(BF16),
        "w_out": w_out[i].astype(BF16),
        "ple_gate": ple_gate[i].astype(BF16),
        "ple_proj": ple_proj[i].astype(BF16),
        "ln_g": ln_g[i].reshape(1, D_MODEL),
        "ln_b": ln_b[i].reshape(1, D_MODEL),
    }


def kernel(x_prompt, x_sample, cache_kv_latent, cache_k_rope, state_pool, page_table, p_prompt, p_sample,
           ln_in_g, ln_in_b, w_in, pool_w, pool_scale, pool_up, q_norm_g, w_uq, kv_norm_g, w_uk, w_uv,
           attn_up, w_out, ple_proj, ple_gate, ln_g, ln_b):
    bsz, seq, _ = x_prompt.shape
    nsmp = x_sample.shape[0]
    n_pages = page_table.shape[1]
    past_len = n_pages * PAGE_SIZE
    n_tok = bsz * seq
    tm_pre, tm_post, bq, bk = 512, 512, 1024, 1024

    tabs_p = _rope_tables(jnp.arange(seq))
    tabs_s = _rope_tables(jnp.full((1,), past_len))
    ln_in = (ln_in_g.reshape(1, D_MODEL), ln_in_b.reshape(1, D_MODEL))
    cache_rope_t = jnp.swapaxes(cache_k_rope, 2, 3)

    hp = x_prompt.reshape(n_tok, D_MODEL)
    hs = x_sample.reshape(nsmp, D_MODEL)

    lat_p, rope_p, pool_p, lat_s, rope_s, pool_s = [], [], [], [], [], []
    for i in range(DEPTH):
        first = i == 0
        wl = _prep_layer(i, w_in, pool_w, pool_scale, pool_up, q_norm_g, w_uq, kv_norm_g, w_uk, w_uv, attn_up,
                         w_out, ple_proj, ple_gate, ln_g, ln_b)
        q, k, vt, lat, krope, gp, sga, ga, pstate = _pre_prompt_call(hp.reshape(bsz, seq, D_MODEL), ln_in, tabs_p,
                                                                     wl, tm_pre, first)
        o = _attn_call(q, k, vt, bq, bk)
        hp = _post_call(o.reshape(n_tok, ATTN_WIDTH), ga.reshape(n_tok, ATTN_WIDTH), gp.reshape(n_tok, D_MODEL),
                        sga.reshape(n_tok, D_MODEL), hp, p_prompt.reshape(DEPTH, n_tok, PLE_DIM), i, ln_in, wl, tm_post,
                        first)
        lat_p.append(lat)
        rope_p.append(krope)
        pool_p.append(pstate[:, 1:, :])
        qs, qlat, lat_n, rope_n, gp_s, sga_s, ga_s, pv_s = _pre_sample_call(
            hs, ln_in, jnp.swapaxes(state_pool[i], 0, 1), tabs_s, wl, first)
        qrope = qs.reshape(nsmp, N_HEADS, HEAD_PAD)[:, :, QK_NOPE:QK_HEAD]
        olat = _decode_call(page_table, qlat.reshape(nsmp, N_HEADS, KV_LORA), qrope,
                            lat_n.reshape(nsmp, 1, KV_LORA), rope_n.reshape(nsmp, 1, QK_ROPE),
                            cache_kv_latent, cache_rope_t, i)
        o_s = _olat_call(olat.reshape(nsmp, N_HEADS * KV_LORA), wl["wuvt"])
        hs = _post_call(o_s, ga_s, gp_s, sga_s, hs, p_sample.reshape(DEPTH, nsmp, PLE_DIM), i, ln_in, wl, nsmp,
                        first)
        lat_s.append(lat_n.reshape(nsmp, 1, KV_LORA))
        rope_s.append(rope_n.reshape(nsmp, 1, QK_ROPE))
        pool_s.append(jnp.concatenate([state_pool[i][:, 1:, :], pv_s[:, None, :]], axis=1))

    return (hp.reshape(bsz, seq, D_MODEL), hs.reshape(nsmp, 1, D_MODEL), jnp.stack(lat_p), jnp.stack(rope_p),
            jnp.stack(pool_p), jnp.stack(lat_s), jnp.stack(rope_s), jnp.stack(pool_s))
```
